```python
import math
import jax, jax.numpy as jnp
from jax import lax
import numpy as np

D_MODEL = 2048
BATCH = 8
SEQ = 2048
DEPTH = 2
DEC_BATCH = 128
DEC_SEQ = 1
PAST_LEN = 2048
PAGE_SIZE = 128

N_MEM = 256
H_DA = 8
DH_DA = 64
DV_DA = 2 * DH_DA
H_RET = 4
DK_RET = 128
DV_RET = 256
H_X = 4
DH_X = 128
D_FF = 4 * D_MODEL
D_DA = H_DA * DV_DA
D_RET = H_RET * DV_RET
D_MIX = D_DA + D_RET
Q_BLOCK = 128
RET_CHUNK = 128
EPS = 1e-5
NEG_INF = -1e30
IN_SPLITS = (H_DA * 2 * DH_DA, H_DA * 2 * DH_DA, D_DA, H_RET * DK_RET, H_RET * DK_RET, D_RET, D_RET)
D_IN = H_DA * 2 * DH_DA * 2 + D_DA + H_RET * DK_RET * 2 + D_RET * 2

kernel_name = "hymba_diffattn_retention_decode_step"


def rmsnorm(x, g):
    xf = x.astype(jnp.float32)
    y = xf * lax.rsqrt(jnp.mean(xf * xf, axis=-1, keepdims=True) + EPS)
    return (y * g.astype(jnp.float32)).astype(x.dtype)


def alibi_slopes():
    return 2.0 ** (-8.0 * (jnp.arange(H_DA, dtype=jnp.float32) + 1.0) / H_DA)


def retention_log_decay():
    return jnp.log(1.0 - 2.0 ** (-5.0 - jnp.arange(H_RET, dtype=jnp.float32)))


def lambda_init(layer):
    return 0.8 - 0.6 * math.exp(-0.3 * layer)


def diff_lambda(lq1, lk1, lq2, lk2, lam_init):
    f = jnp.float32
    return (jnp.exp(jnp.sum(lq1.astype(f) * lk1.astype(f)))
            - jnp.exp(jnp.sum(lq2.astype(f) * lk2.astype(f))) + lam_init)


def project(h, w_in):
    b, s, _ = h.shape
    p = h @ w_in
    bounds = np.cumsum(IN_SPLITS)[:-1].tolist()
    q_da, k_da, v_da, q_r, k_r, v_r, g_r = jnp.split(p, bounds, axis=-1)
    return (q_da.reshape(b, s, H_DA, 2, DH_DA), k_da.reshape(b, s, H_DA, 2, DH_DA),
            v_da.reshape(b, s, H_DA, DV_DA), q_r.reshape(b, s, H_RET, DK_RET),
            k_r.reshape(b, s, H_RET, DK_RET), v_r.reshape(b, s, H_RET, DV_RET), g_r)


def diff_attend(q, k, v, q_pos, k_pos, lam):
    s = jnp.einsum('bqhcd,bkhcd->bhcqk', q, k).astype(jnp.float32) * (DH_DA ** -0.5)
    dist = q_pos[:, None] - k_pos[None, :]
    bias = -alibi_slopes()[:, None, None] * dist.astype(jnp.float32)
    s = jnp.where(dist >= 0, s + bias[None, :, None], NEG_INF)
    p = jax.nn.softmax(s, axis=-1)
    w = p[:, :, 0] - lam * p[:, :, 1]
    return jnp.einsum('bhqk,bkhe->bqhe', w.astype(v.dtype), v)


def diff_attn_prompt(q, k, v, lam):
    b, s = q.shape[:2]
    nb = s // Q_BLOCK
    qb = q.reshape(b, nb, Q_BLOCK, H_DA, 2, DH_DA).swapaxes(0, 1)
    k_pos = jnp.arange(s, dtype=jnp.int32)

    def block(args):
        qi, i = args
        q_pos = i * Q_BLOCK + jnp.arange(Q_BLOCK, dtype=jnp.int32)
        return diff_attend(qi, k, v, q_pos, k_pos, lam)

    ob = lax.map(block, (qb, jnp.arange(nb, dtype=jnp.int32)))
    return ob.swapaxes(0, 1).reshape(b, s, H_DA, DV_DA)


def retention_chunk(S, q, k, v, log_gamma):
    c = q.shape[1]
    n = jnp.arange(c, dtype=jnp.float32)
    diff = n[:, None] - n[None, :]
    dmask = jnp.where(diff >= 0, jnp.exp(log_gamma[:, None, None] * jnp.maximum(diff, 0.0)), 0.0)
    qf = q.astype(jnp.float32)
    kf = k.astype(jnp.float32) * (DK_RET ** -0.5)
    vf = v.astype(jnp.float32)
    inner = jnp.einsum('bnhd,bmhd->bhnm', qf, kf) * dmask[None]
    o_in = jnp.einsum('bhnm,bmhe->bnhe', inner, vf)
    cross_dec = jnp.exp(log_gamma[None, :] * (n[:, None] + 1.0))
    o_cross = jnp.einsum('bnhd,bhde->bnhe', qf, S) * cross_dec[None, :, :, None]
    k_dec = jnp.exp(log_gamma[None, :] * (c - 1.0 - n[:, None]))
    S_new = (jnp.exp(log_gamma * c)[None, :, None, None] * S
             + jnp.einsum('bmhd,bmhe->bhde', kf * k_dec[None, :, :, None], vf))
    return o_in + o_cross, S_new


def retention_prompt(q, k, v, log_gamma):
    b, s = q.shape[:2]
    nc = s // RET_CHUNK

    def to_chunks(t):
        return t.reshape(b, nc, RET_CHUNK, *t.shape[2:]).swapaxes(0, 1)

    S0 = jnp.zeros((b, H_RET, DK_RET, DV_RET), jnp.float32)

    def step(S, xs):
        qc, kc, vc = xs
        o, S = retention_chunk(S, qc, kc, vc, log_gamma)
        return S, o

    S_fin, o = lax.scan(step, S0, (to_chunks(q), to_chunks(k), to_chunks(v)))
    return o.swapaxes(0, 1).reshape(b, s, H_RET, DV_RET), S_fin


def merge_heads(o_da, o_r, g_r, subln, ret_norm, lam_init, w_out, dtype):
    b, s = o_da.shape[:2]
    y_da = (rmsnorm(o_da, subln) * (1.0 - lam_init)).reshape(b, s, D_DA).astype(dtype)
    y_r = (rmsnorm(o_r, ret_norm).reshape(b, s, D_RET) * jax.nn.silu(g_r.astype(jnp.float32))).astype(dtype)
    return jnp.concatenate([y_da, y_r], axis=-1) @ w_out


def mem_kv(mem, g, w_kv):
    b = mem.shape[0]
    kv = rmsnorm(mem, g) @ w_kv
    k, v = jnp.split(kv, 2, axis=-1)
    return k.reshape(b, N_MEM, H_X, DH_X), v.reshape(b, N_MEM, H_X, DH_X)


def cross_attend(h, k, v, wq, wo):
    b, s, _ = h.shape
    q = (h @ wq).reshape(b, s, H_X, DH_X)
    sc = jnp.einsum('bqhd,bkhd->bhqk', q, k).astype(jnp.float32) * (DH_X ** -0.5)
    p = jax.nn.softmax(sc, axis=-1)
    o = jnp.einsum('bhqk,bkhd->bqhd', p.astype(v.dtype), v).reshape(b, s, H_X * DH_X)
    return o @ wo


def sq_relu_mlp(h, w_up, w_down):
    return jnp.square(jax.nn.relu(h @ w_up)) @ w_down


def setup_inputs(seed: int = 0) -> dict:
    key = jax.random.key(seed)
    ks = jax.random.split(key, 32)
    f32 = jnp.float32
    n_pages = PAST_LEN // PAGE_SIZE
    n_used = DEC_BATCH * n_pages
    n_pool = n_used + (n_used + 3) // 4

    def nrm(k, shape, scale=1.0):
        return jax.random.normal(k, shape, f32) * scale

    def gain(k, shape):
        return 1.0 + 0.05 * nrm(k, shape)

    page_table = jax.random.permutation(ks[0], n_pool)[:n_used].reshape(DEC_BATCH, n_pages).astype(jnp.int32)
    return {
        "x_prompt": nrm(ks[1], (BATCH, SEQ, D_MODEL)),
        "x_sample": nrm(ks[2], (DEC_BATCH, DEC_SEQ, D_MODEL)),
        "mem_prompt": nrm(ks[3], (BATCH, N_MEM, D_MODEL)),
        "cache_k_diff": nrm(ks[4], (DEPTH, n_pool, PAGE_SIZE, H_DA, 2, DH_DA)),
        "cache_v_diff": nrm(ks[5], (DEPTH, n_pool, PAGE_SIZE, H_DA, DV_DA)),
        "cache_mem_k": nrm(ks[6], (DEPTH, DEC_BATCH, N_MEM, H_X, DH_X)),
        "cache_mem_v": nrm(ks[7], (DEPTH, DEC_BATCH, N_MEM, H_X, DH_X)),
        "state_ret": nrm(ks[8], (DEPTH, DEC_BATCH, H_RET, DK_RET, DV_RET), 0.5),
        "page_table": page_table,
        "norm_mix": gain(ks[9], (DEPTH, D_MODEL)),
        "w_in": nrm(ks[10], (DEPTH, D_MODEL, D_IN), D_MODEL ** -0.5),
        "w_out": nrm(ks[11], (DEPTH, D_MIX, D_MODEL), D_MIX ** -0.5),
        "da_lq1": nrm(ks[12], (DEPTH, DH_DA), 0.1),
        "da_lk1": nrm(ks[13], (DEPTH, DH_DA), 0.1),
        "da_lq2": nrm(ks[14], (DEPTH, DH_DA), 0.1),
        "da_lk2": nrm(ks[15], (DEPTH, DH_DA), 0.1),
        "da_subln": gain(ks[16], (DEPTH, DV_DA)),
        "ret_norm": gain(ks[17], (DEPTH, DV_RET)),
        "norm_x": gain(ks[18], (DEPTH, D_MODEL)),
        "norm_mem": gain(ks[19], (DEPTH, D_MODEL)),
        "wx_q": nrm(ks[20], (DEPTH, D_MODEL, H_X * DH_X), D_MODEL ** -0.5),
        "wx_kv": nrm(ks[21], (DEPTH, D_MODEL, 2 * H_X * DH_X), D_MODEL ** -0.5),
        "wx_o": nrm(ks[22], (DEPTH, H_X * DH_X, D_MODEL), (H_X * DH_X) ** -0.5),
        "norm_ff": gain(ks[23], (DEPTH, D_MODEL)),
        "w_up": nrm(ks[24], (DEPTH, D_MODEL, D_FF), D_MODEL ** -0.5),
        "w_down": nrm(ks[25], (DEPTH, D_FF, D_MODEL), D_FF ** -0.5),
        "norm_final": gain(ks[26], (D_MODEL,)),
    }


def reference(x_prompt, x_sample, mem_prompt, cache_k_diff, cache_v_diff, cache_mem_k, cache_mem_v,
              state_ret, page_table, norm_mix, w_in, w_out, da_lq1, da_lk1, da_lq2, da_lk2, da_subln,
              ret_norm, norm_x, norm_mem, wx_q, wx_kv, wx_o, norm_ff, w_up, w_down, norm_final):
    log_gamma = retention_log_decay()
    xp, xs = x_prompt, x_sample
    pk, pv, ps, pmk, pmv, sk, sv, ss = [], [], [], [], [], [], [], []
    q_pos_s = PAST_LEN + jnp.arange(DEC_SEQ, dtype=jnp.int32)
    k_pos_s = jnp.arange(PAST_LEN + DEC_SEQ, dtype=jnp.int32)
    for l in range(DEPTH):
        lam_init = lambda_init(l)
        lam = diff_lambda(da_lq1[l], da_lk1[l], da_lq2[l], da_lk2[l], lam_init)

        q_da, k_da, v_da, q_r, k_r, v_r, g_r = project(rmsnorm(xp, norm_mix[l]), w_in[l])
        o_da = diff_attn_prompt(q_da, k_da, v_da, lam)
        o_r, S_p = retention_prompt(q_r, k_r, v_r, log_gamma)
        xp = xp + merge_heads(o_da, o_r, g_r, da_subln[l], ret_norm[l], lam_init, w_out[l], xp.dtype)
        mk, mv = mem_kv(mem_prompt, norm_mem[l], wx_kv[l])
        xp = xp + cross_attend(rmsnorm(xp, norm_x[l]), mk, mv, wx_q[l], wx_o[l])
        xp = xp + sq_relu_mlp(rmsnorm(xp, norm_ff[l]), w_up[l], w_down[l])
        pk.append(k_da)
        pv.append(v_da)
        ps.append(S_p.astype(x_prompt.dtype))
        pmk.append(mk)
        pmv.append(mv)

        q_da, k_da, v_da, q_r, k_r, v_r, g_r = project(rmsnorm(xs, norm_mix[l]), w_in[l])
        k_past = cache_k_diff[l][page_table].reshape(DEC_BATCH, PAST_LEN, H_DA, 2, DH_DA)
        v_past = cache_v_diff[l][page_table].reshape(DEC_BATCH, PAST_LEN, H_DA, DV_DA)
        k_all = jnp.concatenate([k_past, k_da.astype(k_past.dtype)], axis=1)
        v_all = jnp.concatenate([v_past, v_da.astype(v_past.dtype)], axis=1)
        o_da = diff_attend(q_da, k_all, v_all, q_pos_s, k_pos_s, lam)
        o_r, S_s = retention_chunk(state_ret[l].astype(jnp.float32), q_r, k_r, v_r, log_gamma)
        xs = xs + merge_heads(o_da, o_r, g_r, da_subln[l], ret_norm[l], lam_init, w_out[l], xs.dtype)
        xs = xs + cross_attend(rmsnorm(xs, norm_x[l]), cache_mem_k[l], cache_mem_v[l], wx_q[l], wx_o[l])
        xs = xs + sq_relu_mlp(rmsnorm(xs, norm_ff[l]), w_up[l], w_down[l])
        sk.append(k_da.astype(cache_k_diff.dtype))
        sv.append(v_da.astype(cache_v_diff.dtype))
        ss.append(S_s.astype(state_ret.dtype))

    y_prompt = rmsnorm(xp, norm_final)
    y_sample = rmsnorm(xs, norm_final)
    return (y_prompt, y_sample, jnp.stack(pk), jnp.stack(pv), jnp.stack(ps), jnp.stack(pmk), jnp.stack(pmv),
            jnp.stack(sk), jnp.stack(sv), jnp.stack(ss))
```

```python
import functools
import math

import jax
import jax.numpy as jnp
from jax import lax
from jax.experimental import pallas as pl
from jax.experimental.pallas import tpu as pltpu

EPS = 1e-5
NEG_INF = -1e30
F32 = jnp.float32
BF16 = jnp.bfloat16

V7X_VMEM_LIMIT_BYTES = 56 * 1024 * 1024


def _params(*semantics):
    return pltpu.CompilerParams(dimension_semantics=semantics,
                                vmem_limit_bytes=V7X_VMEM_LIMIT_BYTES)


def _pick(n, candidates):
    for c in candidates:
        if n % c == 0:
            return c
    return n


def _dot(a, b):
    return jnp.dot(a, b, preferred_element_type=F32)


def _dot_nt(a, b):
    return lax.dot_general(a, b, (((1,), (1,)), ((), ())), preferred_element_type=F32)


def _rmsnorm_kernel(x_ref, g_ref, o_ref):
    x = x_ref[...]
    ms = jnp.mean(x * x, axis=-1, keepdims=True)
    o_ref[...] = (x * lax.rsqrt(ms + EPS) * g_ref[...]).astype(o_ref.dtype)


def rmsnorm(x, g, out_dtype):
    m, d = x.shape
    tm = _pick(m, (512, 256, 128))
    return pl.pallas_call(
        _rmsnorm_kernel,
        grid=(m // tm,),
        in_specs=[pl.BlockSpec((tm, d), lambda i: (i, 0)),
                  pl.BlockSpec((1, d), lambda i: (0, 0))],
        out_specs=pl.BlockSpec((tm, d), lambda i: (i, 0)),
        out_shape=jax.ShapeDtypeStruct((m, d), out_dtype),
        compiler_params=_params("parallel"),
        name="rmsnorm",
    )(x, g.reshape(1, d))


def _headnorm_kernel(*refs, scale, gated):
    if gated:
        x_ref, g_ref, gate_ref, o_ref = refs
    else:
        x_ref, g_ref, o_ref = refs
    x = x_ref[...]
    ms = jnp.mean(x * x, axis=-1, keepdims=True)
    y = x * lax.rsqrt(ms + EPS) * g_ref[...]
    if gated:
        gate = gate_ref[...]
        y = y * (gate * jax.nn.sigmoid(gate))
    else:
        y = y * scale
    o_ref[...] = y.astype(o_ref.dtype)


def headnorm(x, g, *, scale=1.0, gate=None):
    r, d = x.shape
    tr = _pick(r, (512, 256, 128))
    ins = [x, g.reshape(1, d)]
    specs = [pl.BlockSpec((tr, d), lambda i: (i, 0)), pl.BlockSpec((1, d), lambda i: (0, 0))]
    if gate is not None:
        ins.append(gate)
        specs.append(pl.BlockSpec((tr, d), lambda i: (i, 0)))
    return pl.pallas_call(
        functools.partial(_headnorm_kernel, scale=scale, gated=gate is not None),
        grid=(r // tr,),
        in_specs=specs,
        out_specs=pl.BlockSpec((tr, d), lambda i: (i, 0)),
        out_shape=jax.ShapeDtypeStruct((r, d), BF16),
        compiler_params=_params("parallel"),
        name="headnorm",
    )(*ins)


def _mm_kernel(*refs, n_pairs, has_res):
    acc = _dot(refs[0][...], refs[1][...])
    for p in range(1, n_pairs):
        acc = acc + _dot(refs[2 * p][...], refs[2 * p + 1][...])
    pos = 2 * n_pairs
    if has_res:
        acc = refs[pos][...] + acc
        pos += 1
    for o_ref in refs[pos:]:
        o_ref[...] = acc.astype(o_ref.dtype)


def matmul(pairs, out_dtypes, residual=None):
    m = pairs[0][0].shape[0]
    n = pairs[0][1].shape[1]
    tm = _pick(m, (512, 256, 128))
    tn = _pick(n, (1024, 512, 256, 128))
    ins, specs = [], []
    for a, w in pairs:
        k = a.shape[1]
        ins += [a, w]
        specs += [pl.BlockSpec((tm, k), lambda j, i: (i, 0)),
                  pl.BlockSpec((k, tn), lambda j, i: (0, j))]
    if residual is not None:
        ins.append(residual)
        specs.append(pl.BlockSpec((tm, tn), lambda j, i: (i, j)))
    return pl.pallas_call(
        functools.partial(_mm_kernel, n_pairs=len(pairs), has_res=residual is not None),
        grid=(n // tn, m // tm),
        in_specs=specs,
        out_specs=tuple(pl.BlockSpec((tm, tn), lambda j, i: (i, j)) for _ in out_dtypes),
        out_shape=tuple(jax.ShapeDtypeStruct((m, n), dt) for dt in out_dtypes),
        compiler_params=_params("parallel", "parallel"),
        name="matmul",
    )(*ins)


def _mlp_kernel(h_ref, wu_ref, wd_ref, res_ref, o_ref):
    f = pl.program_id(1)
    u = _dot(h_ref[...], wu_ref[...])
    a = jnp.square(jnp.maximum(u, 0.0)).astype(BF16)
    c = _dot(a, wd_ref[...])

    @pl.when(f == 0)
    def _():
        o_ref[...] = res_ref[...] + c

    @pl.when(f != 0)
    def _():
        o_ref[...] += c


def mlp(h, w_up, w_down, res):
    m, d = h.shape
    ff = w_up.shape[1]
    tm = _pick(m, (512, 256, 128))
    tf = _pick(ff, (512, 256, 128))
    return pl.pallas_call(
        _mlp_kernel,
        grid=(m // tm, ff // tf),
        in_specs=[pl.BlockSpec((tm, d), lambda i, f: (i, 0)),
                  pl.BlockSpec((d, tf), lambda i, f: (0, f)),
                  pl.BlockSpec((tf, d), lambda i, f: (f, 0)),
                  pl.BlockSpec((tm, d), lambda i, f: (i, 0))],
        out_specs=pl.BlockSpec((tm, d), lambda i, f: (i, 0)),
        out_shape=jax.ShapeDtypeStruct((m, d), F32),
        compiler_params=_params("parallel", "arbitrary"),
        name="mlp",
    )(h, w_up, w_down, res)


def _diff_lambda(lq1_ref, lk1_ref, lq2_ref, lk2_ref, lam_init):
    e1 = jnp.exp(jnp.sum(lq1_ref[...] * lk1_ref[...], axis=-1, keepdims=True))
    e2 = jnp.exp(jnp.sum(lq2_ref[...] * lk2_ref[...], axis=-1, keepdims=True))
    return e1 - e2 + lam_init


def _diff_attn_kernel(slopes_ref, q_ref, k_ref, v_ref, lq1_ref, lk1_ref, lq2_ref, lk2_ref,
                      subln_ref, o_ref, m_ref, l_ref, acc_ref, *, t, dh, lam_init):
    h = pl.program_id(1)
    i = pl.program_id(2)
    nslope = -slopes_ref[h]
    row = lax.broadcasted_iota(jnp.int32, (2 * t, t), 0)
    col = lax.broadcasted_iota(jnp.int32, (2 * t, t), 1)
    rel = jnp.where(row >= t, row - t, row) - col
    bias0 = nslope * rel.astype(F32)

    q = q_ref[...].astype(F32) * (dh ** -0.5)
    lane = lax.broadcasted_iota(jnp.int32, q.shape, 1)
    qq = jnp.concatenate([jnp.where(lane < dh, q, 0.0), jnp.where(lane >= dh, q, 0.0)],
                         axis=0).astype(BF16)

    def scores(j):
        kj = k_ref[pl.ds(pl.multiple_of(j * t, t), t), :]
        return _dot_nt(qq, kj)

    def pv(p, j):
        vj = v_ref[pl.ds(pl.multiple_of(j * t, t), t), :]
        return _dot(p.astype(BF16), vj)

    s = jnp.where(rel >= 0, scores(i) + bias0, NEG_INF)
    m0 = jnp.max(s, axis=-1, keepdims=True)
    p = jnp.exp(s - m0)
    m_ref[...] = m0
    l_ref[...] = jnp.sum(p, axis=-1, keepdims=True)
    acc_ref[...] = pv(p, i)

    def body(j, carry):
        s = scores(j) + (bias0 + nslope * ((i - j) * t).astype(F32))
        m_prev = m_ref[...]
        m_new = jnp.maximum(m_prev, jnp.max(s, axis=-1, keepdims=True))
        alpha = jnp.exp(m_prev - m_new)
        p = jnp.exp(s - m_new)
        l_ref[...] = alpha * l_ref[...] + jnp.sum(p, axis=-1, keepdims=True)
        acc_ref[...] = alpha * acc_ref[...] + pv(p, j)
        m_ref[...] = m_new
        return carry

    lax.fori_loop(0, i, body, 0)

    lam = _diff_lambda(lq1_ref, lk1_ref, lq2_ref, lk2_ref, lam_init)
    o = acc_ref[...] / l_ref[...]
    o = o[:t] - lam * o[t:]
    ms = jnp.mean(o * o, axis=-1, keepdims=True)
    y = o * lax.rsqrt(ms + EPS) * subln_ref[...] * (1.0 - lam_init)
    o_ref[...] = y.astype(o_ref.dtype)


def diff_attn_prompt(q, kv, slopes, lparams, subln, *, batch, seq, n_heads, dh, dv, lam_init):
    assert (n_heads * 2 * dh) % dv == 0
    v_blk = n_heads * 2 * dh // dv
    t = _pick(seq, (256, 128))
    nq = seq // t
    vec = lambda n: pl.BlockSpec((1, n), lambda b, h, i: (0, 0))
    return pl.pallas_call(
        functools.partial(_diff_attn_kernel, t=t, dh=dh, lam_init=lam_init),
        grid=(batch, n_heads, nq),
        in_specs=[pl.BlockSpec(memory_space=pltpu.SMEM),
                  pl.BlockSpec((t, 2 * dh), lambda b, h, i: (b * nq + i, h)),
                  pl.BlockSpec((seq, 2 * dh), lambda b, h, i: (b, h)),
                  pl.BlockSpec((seq, dv), lambda b, h, i: (b, v_blk + h)),
                  vec(dh), vec(dh), vec(dh), vec(dh), vec(dv)],
        out_specs=pl.BlockSpec((t, dv), lambda b, h, i: (b * nq + i, h)),
        out_shape=jax.ShapeDtypeStruct((batch * seq, n_heads * dv), BF16),
        scratch_shapes=[pltpu.VMEM((2 * t, 1), F32), pltpu.VMEM((2 * t, 1), F32),
                        pltpu.VMEM((2 * t, dv), F32)],
        compiler_params=_params("parallel", "parallel", "arbitrary"),
        name="diff_attn_prompt",
    )(slopes, q, kv, kv, *lparams, subln)


def _retention_kernel(q_ref, k_ref, v_ref, g_ref, norm_ref, y_ref, s_ref, *, c, n_heads, dk, dv,
                      log_gammas):
    step = pl.program_id(1)

    @pl.when(step == 0)
    def _():
        s_ref[...] = jnp.zeros_like(s_ref)

    row = lax.broadcasted_iota(jnp.int32, (c, c), 0)
    col = lax.broadcasted_iota(jnp.int32, (c, c), 1)
    diff = (row - col).astype(F32)
    n_v = lax.broadcasted_iota(jnp.int32, (c, dv), 0).astype(F32)
    n_k = lax.broadcasted_iota(jnp.int32, (c, dk), 0).astype(F32)
    scale = dk ** -0.5
    for h in range(n_heads):
        lg = log_gammas[h]
        q = q_ref[:, h * dk:(h + 1) * dk]
        k = k_ref[:, h * dk:(h + 1) * dk]
        v = v_ref[:, h * dv:(h + 1) * dv]
        state = s_ref[h]
        dmask = jnp.where(diff >= 0, jnp.exp(lg * jnp.maximum(diff, 0.0)), 0.0) * scale
        inner = _dot_nt(q, k) * dmask
        o = _dot(inner.astype(BF16), v)
        o = o + _dot(q, state.astype(BF16)) * jnp.exp(lg * (n_v + 1.0))
        k_dec = (k.astype(F32) * (jnp.exp(lg * (c - 1.0 - n_k)) * scale)).T
        s_ref[h] = math.exp(lg * c) * state + _dot(k_dec.astype(BF16), v)
        ms = jnp.mean(o * o, axis=-1, keepdims=True)
        gate = g_ref[:, h * dv:(h + 1) * dv]
        y = o * lax.rsqrt(ms + EPS) * norm_ref[...] * (gate * jax.nn.sigmoid(gate))
        y_ref[:, h * dv:(h + 1) * dv] = y.astype(y_ref.dtype)


def retention_prompt(pa, g_r, ret_norm, *, batch, seq, n_heads, dk, dv, chunk, log_gammas,
                     q_col, k_col, v_col):
    nc = seq // chunk
    wq = n_heads * dk
    wv = n_heads * dv
    return pl.pallas_call(
        functools.partial(_retention_kernel, c=chunk, n_heads=n_heads, dk=dk, dv=dv,
                          log_gammas=log_gammas),
        grid=(batch, nc),
        in_specs=[pl.BlockSpec((chunk, wq), lambda b, s: (b * nc + s, q_col // wq)),
                  pl.BlockSpec((chunk, wq), lambda b, s: (b * nc + s, k_col // wq)),
                  pl.BlockSpec((chunk, wv), lambda b, s: (b * nc + s, v_col // wv)),
                  pl.BlockSpec((chunk, wv), lambda b, s: (b * nc + s, 0)),
                  pl.BlockSpec((1, dv), lambda b, s: (0, 0))],
        out_specs=(pl.BlockSpec((chunk, wv), lambda b, s: (b * nc + s, 0)),
                   pl.BlockSpec((None, n_heads, dk, dv), lambda b, s: (b, 0, 0, 0))),
        out_shape=(jax.ShapeDtypeStruct((batch * seq, wv), BF16),
                   jax.ShapeDtypeStruct((batch, n_heads, dk, dv), F32)),
        compiler_params=_params("parallel", "arbitrary"),
        name="retention_prompt",
    )(pa, pa, pa, g_r, ret_norm.reshape(1, dv))


def _cross_attn_kernel(q_ref, k_ref, v_ref, o_ref, *, n_heads, dh):
    for h in range(n_heads):
        sl = slice(h * dh, (h + 1) * dh)
        s = _dot_nt(q_ref[:, sl], k_ref[:, sl]) * (dh ** -0.5)
        m = jnp.max(s, axis=-1, keepdims=True)
        p = jnp.exp(s - m)
        p = p / jnp.sum(p, axis=-1, keepdims=True)
        o_ref[:, sl] = _dot(p.astype(BF16), v_ref[:, sl]).astype(o_ref.dtype)


def cross_attn_prompt(q, mkv, *, batch, seq, n_mem, n_heads, dh):
    w = n_heads * dh
    tq = _pick(seq, (512, 256, 128))
    nq = seq // tq
    return pl.pallas_call(
        functools.partial(_cross_attn_kernel, n_heads=n_heads, dh=dh),
        grid=(batch, nq),
        in_specs=[pl.BlockSpec((tq, w), lambda b, i: (b * nq + i, 0)),
                  pl.BlockSpec((n_mem, w), lambda b, i: (b, 0)),
                  pl.BlockSpec((n_mem, w), lambda b, i: (b, 1))],
        out_specs=pl.BlockSpec((tq, w), lambda b, i: (b * nq + i, 0)),
        out_shape=jax.ShapeDtypeStruct((batch * seq, w), BF16),
        compiler_params=_params("parallel", "parallel"),
        name="cross_attn_prompt",
    )(q, mkv, mkv)


def _decode_attn_kernel(*refs, n_pages, has_new, diff, lam_init, scale):
    pos = 1
    q_ref = refs[pos]; pos += 1
    k_refs = refs[pos:pos + n_pages]; pos += n_pages
    v_refs = refs[pos:pos + n_pages]; pos += n_pages
    seg_ref = refs[pos]; pos += 1
    spread_ref = refs[pos]; pos += 1
    if has_new:
        kn_ref, vn_ref, bias_ref = refs[pos:pos + 3]; pos += 3
    if diff:
        lparams = refs[pos:pos + 4]; pos += 4
        sign_ref = refs[pos]; pos += 1
    o_ref = refs[pos]

    q = q_ref[...] * scale
    seg = seg_ref[...]

    def score(kblk):
        return _dot((kblk * q).astype(BF16), seg)

    s = [score(k_refs[p][...]) for p in range(n_pages)]
    if has_new:
        s = [s[p] + bias_ref[p] for p in range(n_pages)]
        s_new = score(kn_ref[...])
    m = s[0].max(axis=0, keepdims=True)
    for p in range(1, n_pages):
        m = jnp.maximum(m, s[p].max(axis=0, keepdims=True))
    if has_new:
        m = jnp.maximum(m, s_new)
    e = [jnp.exp(sp - m) for sp in s]
    l = e[0].sum(axis=0, keepdims=True)
    for p in range(1, n_pages):
        l = l + e[p].sum(axis=0, keepdims=True)
    if has_new:
        e_new = jnp.exp(s_new - m)
        l = l + e_new
    coef = 1.0 / l
    if diff:
        lam = _diff_lambda(*lparams, lam_init)
        sign = sign_ref[...]
        coef = coef * (sign - (1.0 - sign) * lam)
    spread = spread_ref[...]

    def weighted(ep, vblk):
        w = _dot((ep * coef).astype(BF16), spread)
        return jnp.sum(w * vblk, axis=0, keepdims=True)

    o = weighted(e[0], v_refs[0][...])
    for p in range(1, n_pages):
        o = o + weighted(e[p], v_refs[p][...])
    if has_new:
        o = o + weighted(e_new, vn_ref[...])
    o_ref[...] = o


def _segment_matrices(groups, dg_k, heads, dv):
    per_head = groups // heads
    seg = (jnp.arange(groups * dg_k)[:, None] // dg_k == jnp.arange(groups)[None, :]).astype(BF16)
    spread = (jnp.arange(groups)[:, None] // per_head == jnp.arange(heads * dv)[None, :] // dv).astype(BF16)
    return seg, spread


def diff_attn_sample(q, k_new, v_new, cache_k, cache_v, layer, page_table, slopes_np, lparams, *,
                     n_heads, dh, dv, lam_init):
    nb, n_pages = page_table.shape
    page = cache_k.shape[2]
    wk, wv = n_heads * 2 * dh, n_heads * dv
    groups = 2 * n_heads
    past = n_pages * page
    seg, spread = _segment_matrices(groups, dh, n_heads, dv)
    dist = (past - jnp.arange(past, dtype=jnp.int32)).astype(F32)
    bias = (-jnp.repeat(slopes_np, 2)[None, :] * dist[:, None]).reshape(n_pages, page, groups)
    sign = (jnp.arange(groups) % 2 == 0).astype(F32).reshape(1, groups)

    row = lambda w: pl.BlockSpec((None, 1, w), lambda b, pt: (b, 0, 0))
    const = lambda shape: pl.BlockSpec(shape, lambda b, pt: (0,) * len(shape))

    def page_spec(p, w):
        return pl.BlockSpec((None, None, page, w), lambda b, pt: (layer, pt[b, p], 0, 0))

    in_specs = ([row(wk)] + [page_spec(p, wk) for p in range(n_pages)]
                + [page_spec(p, wv) for p in range(n_pages)]
                + [const(seg.shape), const(spread.shape), row(wk), row(wv), const(bias.shape)]
                + [const((1, dh))] * 4 + [const(sign.shape)])
    out = pl.pallas_call(
        functools.partial(_decode_attn_kernel, n_pages=n_pages, has_new=True, diff=True,
                          lam_init=lam_init, scale=dh ** -0.5),
        grid_spec=pltpu.PrefetchScalarGridSpec(
            num_scalar_prefetch=1, grid=(nb,), in_specs=in_specs,
            out_specs=pl.BlockSpec((None, 1, wv), lambda b, pt: (b, 0, 0))),
        out_shape=jax.ShapeDtypeStruct((nb, 1, wv), F32),
        compiler_params=_params("arbitrary"),
        name="diff_attn_sample",
    )(page_table, q.reshape(nb, 1, wk), *([cache_k] * n_pages), *([cache_v] * n_pages),
      seg, spread, k_new.reshape(nb, 1, wk), v_new.reshape(nb, 1, wv), bias, *lparams, sign)
    return out.reshape(nb, wv)


def cross_attn_sample(q, cache_k, cache_v, layer, *, n_heads, dh):
    nb, w = q.shape
    n_mem = cache_k.shape[2]
    seg, spread = _segment_matrices(n_heads, dh, n_heads, dh)
    dummy_pt = jnp.zeros((1, 1), jnp.int32)
    kv_spec = pl.BlockSpec((None, None, n_mem, w), lambda b, pt: (layer, b, 0, 0))
    const = lambda shape: pl.BlockSpec(shape, lambda b, pt: (0,) * len(shape))
    out = pl.pallas_call(
        functools.partial(_decode_attn_kernel, n_pages=1, has_new=False, diff=False,
                          lam_init=0.0, scale=dh ** -0.5),
        grid_spec=pltpu.PrefetchScalarGridSpec(
            num_scalar_prefetch=1, grid=(nb,),
            in_specs=[pl.BlockSpec((None, 1, w), lambda b, pt: (b, 0, 0)), kv_spec, kv_spec,
                      const(seg.shape), const(spread.shape)],
            out_specs=pl.BlockSpec((None, 1, w), lambda b, pt: (b, 0, 0))),
        out_shape=jax.ShapeDtypeStruct((nb, 1, w), F32),
        compiler_params=_params("arbitrary"),
        name="cross_attn_sample",
    )(dummy_pt, q.reshape(nb, 1, w), cache_k, cache_v, seg, spread)
    return out.reshape(nb, w)


def _retention_step_kernel(q_ref, k_ref, v_ref, s_ref, o_ref, sn_ref, *, n_heads, dk, dv, log_gammas):
    scale = dk ** -0.5
    eye = (lax.broadcasted_iota(jnp.int32, (dk, dk), 0)
           == lax.broadcasted_iota(jnp.int32, (dk, dk), 1))
    for h in range(n_heads):
        gamma = math.exp(log_gammas[h])
        q = q_ref[:, h * dk:(h + 1) * dk]
        k = k_ref[:, h * dk:(h + 1) * dk]
        v = v_ref[:, h * dv:(h + 1) * dv]
        state = s_ref[h]
        inner = jnp.sum(q * k, axis=-1, keepdims=True) * scale
        q8 = jnp.broadcast_to(q, (8, dk)).astype(BF16)
        o_cross = _dot(q8, state.astype(BF16))[0:1] * gamma
        o_ref[:, h * dv:(h + 1) * dv] = inner * v + o_cross
        k_diag = jnp.where(eye, jnp.broadcast_to(k * scale, (dk, dk)), 0.0).astype(BF16)
        v_rows = jnp.broadcast_to(v, (dk, dv)).astype(BF16)
        sn_ref[h] = gamma * state + _dot(k_diag, v_rows)


def retention_sample(q, k, v, state, layer, *, n_heads, dk, dv, log_gammas):
    nb = q.shape[0]
    wq, wv = n_heads * dk, n_heads * dv
    row = lambda w: pl.BlockSpec((None, 1, w), lambda b: (b, 0, 0))
    o, s_new = pl.pallas_call(
        functools.partial(_retention_step_kernel, n_heads=n_heads, dk=dk, dv=dv,
                          log_gammas=log_gammas),
        grid=(nb,),
        in_specs=[row(wq), row(wq), row(wv),
                  pl.BlockSpec((None, None, n_heads, dk, dv), lambda b: (layer, b, 0, 0, 0))],
        out_specs=(row(wv), pl.BlockSpec((None, n_heads, dk, dv), lambda b: (b, 0, 0, 0))),
        out_shape=(jax.ShapeDtypeStruct((nb, 1, wv), F32),
                   jax.ShapeDtypeStruct((nb, n_heads, dk, dv), F32)),
        compiler_params=_params("parallel"),
        name="retention_sample",
    )(q.reshape(nb, 1, wq), k.reshape(nb, 1, wq), v.reshape(nb, 1, wv), state)
    return o.reshape(nb, wv), s_new


def kernel(x_prompt, x_sample, mem_prompt, cache_k_diff, cache_v_diff, cache_mem_k, cache_mem_v,
           state_ret, page_table, norm_mix, w_in, w_out, da_lq1, da_lk1, da_lq2, da_lk2, da_subln,
           ret_norm, norm_x, norm_mem, wx_q, wx_kv, wx_o, norm_ff, w_up, w_down, norm_final):
    batch, seq, d_model = x_prompt.shape
    dec_batch, dec_seq, _ = x_sample.shape
    assert dec_seq == 1
    depth, n_pool, page, h_da, _, dh_da = cache_k_diff.shape
    dv_da = cache_v_diff.shape[-1]
    _, _, h_ret, dk_ret, dv_ret = state_ret.shape
    _, _, n_mem, h_x, dh_x = cache_mem_k.shape
    d_qk = h_da * 2 * dh_da
    d_da = h_da * dv_da
    d_rqk = h_ret * dk_ret
    d_ret = h_ret * dv_ret
    d_x = h_x * dh_x
    chunk = 128
    mp = batch * seq

    slopes = 2.0 ** (-8.0 * (jnp.arange(h_da, dtype=F32) + 1.0) / h_da)
    log_gammas = tuple(math.log(1.0 - 2.0 ** (-5.0 - h)) for h in range(h_ret))

    o_q, o_k, o_v = 0, d_qk, 2 * d_qk
    o_qr = o_v + d_da
    o_kr, o_vr = o_qr + d_rqk, o_qr + 2 * d_rqk
    o_g = o_vr + d_ret

    cache_k = cache_k_diff.reshape(depth, n_pool, page, d_qk)
    cache_v = cache_v_diff.reshape(depth, n_pool, page, d_da)
    mem_k = cache_mem_k.reshape(depth, dec_batch, n_mem, d_x)
    mem_v = cache_mem_v.reshape(depth, dec_batch, n_mem, d_x)

    xp = x_prompt.reshape(mp, d_model)
    xs = x_sample.reshape(dec_batch, d_model)
    mem = mem_prompt.reshape(batch * n_mem, d_model)

    outs = {name: [] for name in ("pk", "pv", "ps", "pmk", "pmv", "sk", "sv", "ss")}
    for l in range(depth):
        lam_init = 0.8 - 0.6 * math.exp(-0.3 * l)
        lparams = tuple(a[l].reshape(1, dh_da) for a in (da_lq1, da_lk1, da_lq2, da_lk2))
        w_in_l = w_in[l]
        w_a = jnp.concatenate([w_in_l[:, o_q:o_k], w_in_l[:, o_qr:o_g]], axis=1).astype(BF16)
        w_kv = w_in_l[:, o_k:o_qr].astype(BF16)
        w_g = w_in_l[:, o_g:].astype(BF16)
        w_out_da = w_out[l, :d_da].astype(BF16)
        w_out_r = w_out[l, d_da:].astype(BF16)
        wq_b, wkv_b, wo_b = wx_q[l].astype(BF16), wx_kv[l].astype(BF16), wx_o[l].astype(BF16)
        wu_b, wd_b = w_up[l].astype(BF16), w_down[l].astype(BF16)
        a_q, a_qr, a_kr, a_vr = 0, d_qk, d_qk + d_rqk, d_qk + 2 * d_rqk

        h = rmsnorm(xp, norm_mix[l], BF16)
        (pa,) = matmul([(h, w_a)], (BF16,))
        kv_f32, kv_b = matmul([(h, w_kv)], (F32, BF16))
        (g_r,) = matmul([(h, w_g)], (F32,))
        y_da = diff_attn_prompt(pa, kv_b, slopes, lparams, da_subln[l].reshape(1, dv_da),
                                batch=batch, seq=seq, n_heads=h_da, dh=dh_da, dv=dv_da, lam_init=lam_init)
        y_r, s_p = retention_prompt(pa, g_r, ret_norm[l], batch=batch, seq=seq, n_heads=h_ret,
                                    dk=dk_ret, dv=dv_ret, chunk=chunk, log_gammas=log_gammas,
                                    q_col=a_qr, k_col=a_kr, v_col=a_vr)
        (xp,) = matmul([(y_da, w_out_da), (y_r, w_out_r)], (F32,), residual=xp)
        hm = rmsnorm(mem, norm_mem[l], BF16)
        mkv_f32, mkv_b = matmul([(hm, wkv_b)], (F32, BF16))
        hx = rmsnorm(xp, norm_x[l], BF16)
        (qx,) = matmul([(hx, wq_b)], (BF16,))
        ox = cross_attn_prompt(qx, mkv_b, batch=batch, seq=seq, n_mem=n_mem, n_heads=h_x, dh=dh_x)
        (xp,) = matmul([(ox, wo_b)], (F32,), residual=xp)
        hf = rmsnorm(xp, norm_ff[l], BF16)
        xp = mlp(hf, wu_b, wd_b, xp)
        outs["pk"].append(kv_f32[:, :d_qk].reshape(batch, seq, h_da, 2, dh_da))
        outs["pv"].append(kv_f32[:, d_qk:].reshape(batch, seq, h_da, dv_da))
        outs["ps"].append(s_p)
        outs["pmk"].append(mkv_f32[:, :d_x].reshape(batch, n_mem, h_x, dh_x))
        outs["pmv"].append(mkv_f32[:, d_x:].reshape(batch, n_mem, h_x, dh_x))

        h = rmsnorm(xs, norm_mix[l], BF16)
        (ps_a,) = matmul([(h, w_a)], (F32,))
        (kv_s,) = matmul([(h, w_kv)], (F32,))
        (g_s,) = matmul([(h, w_g)], (F32,))
        k_new, v_new = kv_s[:, :d_qk], kv_s[:, d_qk:]
        o_da = diff_attn_sample(ps_a[:, a_q:a_qr], k_new, v_new, cache_k, cache_v, l, page_table,
                                slopes, lparams, n_heads=h_da, dh=dh_da, dv=dv_da, lam_init=lam_init)
        o_r, s_s = retention_sample(ps_a[:, a_qr:a_kr], ps_a[:, a_kr:a_vr], ps_a[:, a_vr:], state_ret, l,
                                    n_heads=h_ret, dk=dk_ret, dv=dv_ret, log_gammas=log_gammas)
        y_da = headnorm(o_da.reshape(dec_batch * h_da, dv_da), da_subln[l], scale=1.0 - lam_init)
        y_r = headnorm(o_r.reshape(dec_batch * h_ret, dv_ret), ret_norm[l],
                       gate=g_s.reshape(dec_batch * h_ret, dv_ret))
        (xs,) = matmul([(y_da.reshape(dec_batch, d_da), w_out_da), (y_r.reshape(dec_batch, d_ret), w_out_r)],
                       (F32,), residual=xs)
        hx = rmsnorm(xs, norm_x[l], BF16)
        (qx,) = matmul([(hx, wq_b)], (F32,))
        ox = cross_attn_sample(qx, mem_k, mem_v, l, n_heads=h_x, dh=dh_x)
        (xs,) = matmul([(ox.astype(BF16), wo_b)], (F32,), residual=xs)
        hf = rmsnorm(xs, norm_ff[l], BF16)
        xs = mlp(hf, wu_b, wd_b, xs)
        outs["sk"].append(k_new.reshape(dec_batch, 1, h_da, 2, dh_da))
        outs["sv"].append(v_new.reshape(dec_batch, 1, h_da, dv_da))
        outs["ss"].append(s_s)

    y_prompt = rmsnorm(xp, norm_final, F32).reshape(batch, seq, d_model)
    y_sample = rmsnorm(xs, norm_final, F32).reshape(dec_batch, 1, d_model)
    st = lambda name: jnp.stack(outs[name])
    return (y_prompt, y_sample, st("pk"), st("pv"), st("ps"), st("pmk"), st("pmv"),
            st("sk"), st("sv"), st("ss"))
```

```python
import functools
import math

import jax
import jax.numpy as jnp
from jax import lax
from jax.experimental import pallas as pl
from jax.experimental.pallas import tpu as pltpu

EPS = 1e-5
NEG_INF = -1e30
LOG2E = math.log2(math.e)
F32 = jnp.float32
BF16 = jnp.bfloat16

V7X_VMEM_LIMIT_BYTES = 56 * 1024 * 1024
V7X_LANES = 128
V7X_SUBLANES = 8


def _params(*semantics):
    return pltpu.CompilerParams(dimension_semantics=semantics,
                                vmem_limit_bytes=V7X_VMEM_LIMIT_BYTES)


def _pick(n, candidates):
    for c in candidates:
        if n % c == 0:
            return c
    return n


def _dot(a, b):
    return jnp.dot(a, b, preferred_element_type=F32)


def _dot_nt(a, b):
    return lax.dot_general(a, b, (((1,), (1,)), ((), ())), preferred_element_type=F32)


def _rms(x, g):
    ms = jnp.mean(x * x, axis=-1, keepdims=True)
    return x * lax.rsqrt(ms + EPS) * g


def _rmsnorm_kernel(x_ref, g_ref, o_ref):
    o_ref[...] = _rms(x_ref[...], g_ref[...]).astype(o_ref.dtype)


def rmsnorm(x, g, out_dtype):
    m, d = x.shape
    tm = _pick(m, (512, 256, 128))
    return pl.pallas_call(
        _rmsnorm_kernel,
        grid=(m // tm,),
        in_specs=[pl.BlockSpec((tm, d), lambda i: (i, 0)),
                  pl.BlockSpec((1, d), lambda i: (0, 0))],
        out_specs=pl.BlockSpec((tm, d), lambda i: (i, 0)),
        out_shape=jax.ShapeDtypeStruct((m, d), out_dtype),
        compiler_params=_params("parallel"),
        name="rmsnorm",
    )(x, g.reshape(1, d))


def _gated_headnorm_kernel(x_ref, g_ref, gate_ref, o_ref):
    gate = gate_ref[...]
    y = _rms(x_ref[...], g_ref[...]) * (gate * jax.nn.sigmoid(gate))
    o_ref[...] = y.astype(o_ref.dtype)


def gated_headnorm(x, g, gate):
    r, d = x.shape
    tr = _pick(r, (512, 256, 128))
    blk = pl.BlockSpec((tr, d), lambda i: (i, 0))
    return pl.pallas_call(
        _gated_headnorm_kernel,
        grid=(r // tr,),
        in_specs=[blk, pl.BlockSpec((1, d), lambda i: (0, 0)), blk],
        out_specs=blk,
        out_shape=jax.ShapeDtypeStruct((r, d), BF16),
        compiler_params=_params("parallel"),
        name="gated_headnorm",
    )(x, g.reshape(1, d), gate)


def _mm_kernel(*refs, n_pairs, has_res):
    acc = _dot(refs[0][...], refs[1][...])
    for p in range(1, n_pairs):
        acc = acc + _dot(refs[2 * p][...], refs[2 * p + 1][...])
    pos = 2 * n_pairs
    if has_res:
        acc = refs[pos][...] + acc
        pos += 1
    for o_ref in refs[pos:]:
        o_ref[...] = acc.astype(o_ref.dtype)


def matmul(pairs, out_dtypes, residual=None):
    m = pairs[0][0].shape[0]
    n = pairs[0][1].shape[1]
    tm = _pick(m, (512, 256, 128))
    tn = _pick(n, (1024, 512, 256, 128))
    ins, specs = [], []
    for a, w in pairs:
        k = a.shape[1]
        ins += [a, w]
        specs += [pl.BlockSpec((tm, k), lambda j, i: (i, 0)),
                  pl.BlockSpec((k, tn), lambda j, i: (0, j))]
    if residual is not None:
        ins.append(residual)
        specs.append(pl.BlockSpec((tm, tn), lambda j, i: (i, j)))
    return pl.pallas_call(
        functools.partial(_mm_kernel, n_pairs=len(pairs), has_res=residual is not None),
        grid=(n // tn, m // tm),
        in_specs=specs,
        out_specs=tuple(pl.BlockSpec((tm, tn), lambda j, i: (i, j)) for _ in out_dtypes),
        out_shape=tuple(jax.ShapeDtypeStruct((m, n), dt) for dt in out_dtypes),
        compiler_params=_params("parallel", "parallel"),
        name="matmul",
    )(*ins)


def _mlp_kernel(x_ref, g_ref, wu_ref, wd_ref, o_ref, h_ref, *, sub):
    f = pl.program_id(1)

    @pl.when(f == 0)
    def _():
        h_ref[...] = _rms(x_ref[...], g_ref[...]).astype(BF16)

    h = h_ref[...]
    c = None
    for s in range(0, wu_ref.shape[1], sub):
        u = _dot(h, wu_ref[:, s:s + sub])
        a = jnp.square(jnp.maximum(u, 0.0)).astype(BF16)
        d = _dot(a, wd_ref[s:s + sub, :])
        c = d if c is None else c + d

    @pl.when(f == 0)
    def _():
        o_ref[...] = x_ref[...] + c

    @pl.when(f != 0)
    def _():
        o_ref[...] += c


def mlp(x, g, w_up, w_down):
    m, d = x.shape
    ff = w_up.shape[1]
    tm = _pick(m, (512, 256, 128))
    tf = _pick(ff, (1024, 512, 256, 128))
    sub = _pick(tf, (256, 128))
    return pl.pallas_call(
        functools.partial(_mlp_kernel, sub=sub),
        grid=(m // tm, ff // tf),
        in_specs=[pl.BlockSpec((tm, d), lambda i, f: (i, 0)),
                  pl.BlockSpec((1, d), lambda i, f: (0, 0)),
                  pl.BlockSpec((d, tf), lambda i, f: (0, f)),
                  pl.BlockSpec((tf, d), lambda i, f: (f, 0))],
        out_specs=pl.BlockSpec((tm, d), lambda i, f: (i, 0)),
        out_shape=jax.ShapeDtypeStruct((m, d), F32),
        scratch_shapes=[pltpu.VMEM((tm, d), BF16)],
        compiler_params=_params("parallel", "arbitrary"),
        name="mlp",
    )(x, g.reshape(1, d), w_up, w_down)


def _diff_lambda(lq1_ref, lk1_ref, lq2_ref, lk2_ref, lam_init):
    e1 = jnp.exp(jnp.sum(lq1_ref[...] * lk1_ref[...], axis=-1, keepdims=True))
    e2 = jnp.exp(jnp.sum(lq2_ref[...] * lk2_ref[...], axis=-1, keepdims=True))
    return e1 - e2 + lam_init


def _lane_fold(x, op):
    out = x[:, :V7X_LANES]
    for c in range(V7X_LANES, x.shape[1], V7X_LANES):
        out = op(out, x[:, c:c + V7X_LANES])
    return out


def _diff_attn_kernel(slopes_ref, q_ref, k_ref, v_ref, lq1_ref, lk1_ref, lq2_ref, lk2_ref,
                      subln_ref, o_ref, s_ref, m_ref, l_ref, acc_ref, *, t, dh, lam_init):
    h = pl.program_id(1)
    i = pl.program_id(2)
    slope2 = slopes_ref[h] * LOG2E
    kpos0 = lax.broadcasted_iota(jnp.int32, (1, t), 1).astype(F32)

    q = q_ref[...].astype(F32) * (dh ** -0.5 * LOG2E)
    lane = lax.broadcasted_iota(jnp.int32, q.shape, 1)
    qq = jnp.concatenate([jnp.where(lane < dh, q, 0.0), jnp.where(lane >= dh, q, 0.0)],
                         axis=0).astype(BF16)

    def scores(j):
        start = pl.multiple_of(j * t, t)
        kpos = kpos0 + (j * t).astype(F32)
        return _dot_nt(qq, k_ref[pl.ds(start, t), :]) + slope2 * kpos

    def pass1(j, carry):
        s = scores(j)
        s_ref[:, pl.ds(pl.multiple_of(j * t, t), t)] = s
        m_ref[...] = jnp.maximum(m_ref[...], _lane_fold(s, jnp.maximum))
        return carry

    row = lax.broadcasted_iota(jnp.int32, (2 * t, t), 0)
    col = lax.broadcasted_iota(jnp.int32, (2 * t, t), 1)
    visible = jnp.where(row >= t, row - t, row) >= col
    s = jnp.where(visible, scores(i), NEG_INF)
    s_ref[:, pl.ds(pl.multiple_of(i * t, t), t)] = s
    m_ref[...] = _lane_fold(s, jnp.maximum)
    lax.fori_loop(0, i, pass1, 0)

    m = jnp.max(m_ref[...], axis=-1, keepdims=True)
    m_ref[...] = jnp.broadcast_to(m, m_ref.shape)
    l_ref[...] = jnp.zeros_like(l_ref)
    acc_ref[...] = jnp.zeros_like(acc_ref)

    def pass2(j, carry):
        start = pl.multiple_of(j * t, t)
        mb = m_ref[...]
        ps = []
        lsum = l_ref[...]
        for c in range(0, t, V7X_LANES):
            p = jnp.exp2(s_ref[:, pl.ds(start + c, V7X_LANES)] - mb)
            lsum = lsum + p
            ps.append(p.astype(BF16))
        l_ref[...] = lsum
        acc_ref[...] += _dot(jnp.concatenate(ps, axis=1), v_ref[pl.ds(start, t), :])
        return carry

    lax.fori_loop(0, i + 1, pass2, 0)

    lam = _diff_lambda(lq1_ref, lk1_ref, lq2_ref, lk2_ref, lam_init)
    o = acc_ref[...] / jnp.sum(l_ref[...], axis=-1, keepdims=True)
    o = o[:t] - lam * o[t:]
    o_ref[...] = (_rms(o, subln_ref[...]) * (1.0 - lam_init)).astype(o_ref.dtype)


def diff_attn_prompt(q, kv, slopes, lparams, subln, *, batch, seq, n_heads, dh, dv, lam_init):
    assert (n_heads * 2 * dh) % dv == 0
    v_blk = n_heads * 2 * dh // dv
    t = _pick(seq, (512, 256, 128))
    nq = seq // t
    vec = lambda n: pl.BlockSpec((1, n), lambda b, h, i: (0, 0))
    return pl.pallas_call(
        functools.partial(_diff_attn_kernel, t=t, dh=dh, lam_init=lam_init),
        grid=(batch, n_heads, nq),
        in_specs=[pl.BlockSpec(memory_space=pltpu.SMEM),
                  pl.BlockSpec((t, 2 * dh), lambda b, h, i: (b * nq + i, h)),
                  pl.BlockSpec((seq, 2 * dh), lambda b, h, i: (b, h)),
                  pl.BlockSpec((seq, dv), lambda b, h, i: (b, v_blk + h)),
                  vec(dh), vec(dh), vec(dh), vec(dh), vec(dv)],
        out_specs=pl.BlockSpec((t, dv), lambda b, h, i: (b * nq + i, h)),
        out_shape=jax.ShapeDtypeStruct((batch * seq, n_heads * dv), BF16),
        scratch_shapes=[pltpu.VMEM((2 * t, seq), F32),
                        pltpu.VMEM((2 * t, V7X_LANES), F32),
                        pltpu.VMEM((2 * t, V7X_LANES), F32),
                        pltpu.VMEM((2 * t, dv), F32)],
        compiler_params=_params("parallel", "parallel", "arbitrary"),
        name="diff_attn_prompt",
    )(slopes, q, kv, kv, *lparams, subln)


def _retention_kernel(q_ref, k_ref, v_ref, g_ref, norm_ref, y_ref, s_ref, *, c, n_heads, dk, dv,
                      log_gammas):
    step = pl.program_id(1)

    @pl.when(step == 0)
    def _():
        s_ref[...] = jnp.zeros_like(s_ref)

    row = lax.broadcasted_iota(jnp.int32, (c, c), 0)
    col = lax.broadcasted_iota(jnp.int32, (c, c), 1)
    diff = (row - col).astype(F32)
    n_v = lax.broadcasted_iota(jnp.int32, (c, dv), 0).astype(F32)
    n_k = lax.broadcasted_iota(jnp.int32, (c, dk), 0).astype(F32)
    scale = dk ** -0.5
    for h in range(n_heads):
        lg = log_gammas[h]
        q = q_ref[:, h * dk:(h + 1) * dk]
        k = k_ref[:, h * dk:(h + 1) * dk]
        v = v_ref[:, h * dv:(h + 1) * dv]
        state = s_ref[h]
        dmask = jnp.where(diff >= 0, jnp.exp(lg * jnp.maximum(diff, 0.0)), 0.0) * scale
        inner = _dot_nt(q, k) * dmask
        o = _dot(inner.astype(BF16), v)
        o = o + _dot(q, state.astype(BF16)) * jnp.exp(lg * (n_v + 1.0))
        k_dec = (k.astype(F32) * (jnp.exp(lg * (c - 1.0 - n_k)) * scale)).T
        s_ref[h] = math.exp(lg * c) * state + _dot(k_dec.astype(BF16), v)
        gate = g_ref[:, h * dv:(h + 1) * dv]
        y = _rms(o, norm_ref[...]) * (gate * jax.nn.sigmoid(gate))
        y_ref[:, h * dv:(h + 1) * dv] = y.astype(y_ref.dtype)


def retention_prompt(pa, g_r, ret_norm, *, batch, seq, n_heads, dk, dv, chunk, log_gammas,
                     q_col, k_col, v_col):
    nc = seq // chunk
    wq = n_heads * dk
    wv = n_heads * dv
    return pl.pallas_call(
        functools.partial(_retention_kernel, c=chunk, n_heads=n_heads, dk=dk, dv=dv,
                          log_gammas=log_gammas),
        grid=(batch, nc),
        in_specs=[pl.BlockSpec((chunk, wq), lambda b, s: (b * nc + s, q_col // wq)),
                  pl.BlockSpec((chunk, wq), lambda b, s: (b * nc + s, k_col // wq)),
                  pl.BlockSpec((chunk, wv), lambda b, s: (b * nc + s, v_col // wv)),
                  pl.BlockSpec((chunk, wv), lambda b, s: (b * nc + s, 0)),
                  pl.BlockSpec((1, dv), lambda b, s: (0, 0))],
        out_specs=(pl.BlockSpec((chunk, wv), lambda b, s: (b * nc + s, 0)),
                   pl.BlockSpec((None, n_heads, dk, dv), lambda b, s: (b, 0, 0, 0))),
        out_shape=(jax.ShapeDtypeStruct((batch * seq, wv), BF16),
                   jax.ShapeDtypeStruct((batch, n_heads, dk, dv), F32)),
        compiler_params=_params("parallel", "arbitrary"),
        name="retention_prompt",
    )(pa, pa, pa, g_r, ret_norm.reshape(1, dv))


def _cross_attn_kernel(q_ref, k_ref, v_ref, o_ref, *, n_heads, dh):
    for h in range(n_heads):
        sl = slice(h * dh, (h + 1) * dh)
        s = _dot_nt(q_ref[:, sl], k_ref[:, sl]) * (dh ** -0.5)
        m = jnp.max(s, axis=-1, keepdims=True)
        p = jnp.exp(s - m)
        p = p / jnp.sum(p, axis=-1, keepdims=True)
        o_ref[:, sl] = _dot(p.astype(BF16), v_ref[:, sl]).astype(o_ref.dtype)


def cross_attn_prompt(q, mkv, *, batch, seq, n_mem, n_heads, dh):
    w = n_heads * dh
    tq = _pick(seq, (512, 256, 128))
    nq = seq // tq
    return pl.pallas_call(
        functools.partial(_cross_attn_kernel, n_heads=n_heads, dh=dh),
        grid=(batch, nq),
        in_specs=[pl.BlockSpec((tq, w), lambda b, i: (b * nq + i, 0)),
                  pl.BlockSpec((n_mem, w), lambda b, i: (b, 0)),
                  pl.BlockSpec((n_mem, w), lambda b, i: (b, 1))],
        out_specs=pl.BlockSpec((tq, w), lambda b, i: (b * nq + i, 0)),
        out_shape=jax.ShapeDtypeStruct((batch * seq, w), BF16),
        compiler_params=_params("parallel", "parallel"),
        name="cross_attn_prompt",
    )(q, mkv, mkv)


def _diff_decode_kernel(*refs, n_pages, page, n_heads, dh, lam_init):
    pos = 1
    qt_ref, knt_ref, vn_ref = refs[pos:pos + 3]; pos += 3
    kt_refs = refs[pos:pos + n_pages]; pos += n_pages
    v_refs = refs[pos:pos + n_pages]; pos += n_pages
    slopes_ref = refs[pos]; pos += 1
    lparams = refs[pos:pos + 4]; pos += 4
    subln_ref = refs[pos]; pos += 1
    o_ref = refs[pos]

    b = pl.program_id(0)
    nb = qt_ref.shape[1]
    onehot = (lax.broadcasted_iota(jnp.int32, (nb, V7X_LANES), 0) == b).astype(BF16)
    q_rep = _dot(qt_ref[...], onehot) * (dh ** -0.5)
    kn_rep = _dot(knt_ref[...], onehot)

    def map_sums(x):
        x3 = x.reshape(n_heads, 2 * dh, x.shape[-1])
        return x3[:, :dh, :].sum(axis=1), x3[:, dh:, :].sum(axis=1)

    past = n_pages * page
    tpos = lax.broadcasted_iota(jnp.int32, (n_heads, page), 1).astype(F32)
    nslope = -slopes_ref[...]
    s1, s2 = [], []
    for p in range(n_pages):
        a, c = map_sums(kt_refs[p][...] * q_rep)
        bias = nslope * ((past - p * page) - tpos)
        s1.append(a + bias)
        s2.append(c + bias)
    n1, n2 = map_sums(kn_rep * q_rep)

    def softmax_parts(s_list, s_new):
        m = s_list[0]
        for s in s_list[1:]:
            m = jnp.maximum(m, s)
        m = jnp.maximum(jnp.max(m, axis=-1, keepdims=True), s_new)
        e = [jnp.exp(s - m) for s in s_list]
        e_new = jnp.exp(s_new - m)
        l = e[0]
        for x in e[1:]:
            l = l + x
        l = jnp.sum(l, axis=-1, keepdims=True) + e_new
        return e, e_new, 1.0 / l

    e1, en1, a1 = softmax_parts(s1, n1)
    e2, en2, a2 = softmax_parts(s2, n2)
    a2 = a2 * _diff_lambda(*lparams, lam_init)

    tok = lax.broadcasted_iota(jnp.int32, (page, n_heads, V7X_LANES), 0)
    lane = lax.broadcasted_iota(jnp.int32, (page, n_heads, V7X_LANES), 2)
    diag = tok == lane
    ones = jnp.ones((V7X_LANES, V7X_LANES), BF16)
    o = (en1 * a1 - en2 * a2) * vn_ref[...]
    for p in range(n_pages):
        w = e1[p] * a1 - e2[p] * a2
        wd = jnp.where(diag, w[None], 0.0).reshape(page * n_heads, V7X_LANES).astype(BF16)
        w3 = _dot(wd, ones).reshape(page, n_heads, V7X_LANES)
        o = o + jnp.sum(w3 * v_refs[p][...], axis=0)
    o_ref[...] = (_rms(o, subln_ref[...]) * (1.0 - lam_init)).astype(o_ref.dtype)


def diff_attn_sample(q, k_new, v_new, cache_kt, cache_v, layer, page_table, slopes, lparams, subln, *,
                     n_heads, dh, dv, lam_init):
    nb, n_pages = page_table.shape
    page = cache_v.shape[2]
    wk = n_heads * 2 * dh
    assert page == V7X_LANES and dv == V7X_LANES and n_heads == V7X_SUBLANES
    const = lambda shape: pl.BlockSpec(shape, lambda b, pt: (0,) * len(shape))

    def kt_spec(p):
        return pl.BlockSpec((None, None, wk, page), lambda b, pt: (layer, pt[b, p], 0, 0))

    def v_spec(p):
        return pl.BlockSpec((None, None, page, n_heads, dv), lambda b, pt: (layer, pt[b, p], 0, 0, 0))

    in_specs = ([const((wk, nb)), const((wk, nb)),
                 pl.BlockSpec((None, n_heads, dv), lambda b, pt: (b, 0, 0))]
                + [kt_spec(p) for p in range(n_pages)] + [v_spec(p) for p in range(n_pages)]
                + [const((n_heads, V7X_LANES))] + [const((1, dh))] * 4 + [const((1, dv))])
    out = pl.pallas_call(
        functools.partial(_diff_decode_kernel, n_pages=n_pages, page=page, n_heads=n_heads, dh=dh,
                          lam_init=lam_init),
        grid_spec=pltpu.PrefetchScalarGridSpec(
            num_scalar_prefetch=1, grid=(nb,), in_specs=in_specs,
            out_specs=pl.BlockSpec((None, n_heads, dv), lambda b, pt: (b, 0, 0))),
        out_shape=jax.ShapeDtypeStruct((nb, n_heads, dv), BF16),
        compiler_params=_params("arbitrary"),
        name="diff_attn_sample",
    )(page_table, q.T.astype(BF16), k_new.T.astype(BF16), v_new.reshape(nb, n_heads, dv),
      *([cache_kt] * n_pages), *([cache_v] * n_pages),
      jnp.broadcast_to(slopes[:, None], (n_heads, V7X_LANES)), *lparams, subln)
    return out.reshape(nb, n_heads * dv)


def _cross_decode_kernel(q_ref, k_ref, v_ref, o_ref, *, n_heads, dh):
    n_tiles = k_ref.shape[0]
    q = q_ref[...] * (dh ** -0.5)
    ones = jnp.ones((dh, V7X_LANES), BF16)
    prod = (k_ref[...] * q[None]).reshape(n_tiles * V7X_SUBLANES, dh).astype(BF16)
    s = _dot(prod, ones).reshape(n_tiles, V7X_SUBLANES, V7X_LANES)

    def over_tokens(x, op):
        out = x
        for r in range(n_heads, V7X_SUBLANES, n_heads):
            out = op(out, pltpu.roll(x, r, axis=0))
        return out

    m = over_tokens(jnp.max(s, axis=0), jnp.maximum)
    e = jnp.exp(s - m[None])
    l = over_tokens(jnp.sum(e, axis=0), jnp.add)
    o = over_tokens(jnp.sum(e * v_ref[...], axis=0), jnp.add)
    o_ref[...] = o / l


def cross_attn_sample(q, cache_k, cache_v, layer, *, n_heads, dh):
    nb = q.shape[0]
    assert dh == V7X_LANES and V7X_SUBLANES % n_heads == 0
    n_tiles = cache_k.shape[2]
    q8 = jnp.tile(q.reshape(nb, n_heads, dh), (1, V7X_SUBLANES // n_heads, 1))
    kv_spec = pl.BlockSpec((None, None, n_tiles, V7X_SUBLANES, dh), lambda b: (layer, b, 0, 0, 0))
    row = pl.BlockSpec((None, V7X_SUBLANES, dh), lambda b: (b, 0, 0))
    out = pl.pallas_call(
        functools.partial(_cross_decode_kernel, n_heads=n_heads, dh=dh),
        grid=(nb,),
        in_specs=[row, kv_spec, kv_spec],
        out_specs=row,
        out_shape=jax.ShapeDtypeStruct((nb, V7X_SUBLANES, dh), F32),
        compiler_params=_params("parallel"),
        name="cross_attn_sample",
    )(q8, cache_k, cache_v)
    return out[:, :n_heads].reshape(nb, n_heads * dh)


def _retention_step_kernel(q_ref, k_ref, v_ref, s_ref, o_ref, sn_ref, *, n_heads, dk, dv, log_gammas):
    scale = dk ** -0.5
    eye = (lax.broadcasted_iota(jnp.int32, (dk, dk), 0)
           == lax.broadcasted_iota(jnp.int32, (dk, dk), 1))
    for h in range(n_heads):
        gamma = math.exp(log_gammas[h])
        q = q_ref[:, h * dk:(h + 1) * dk]
        k = k_ref[:, h * dk:(h + 1) * dk]
        v = v_ref[:, h * dv:(h + 1) * dv]
        state = s_ref[h]
        inner = jnp.sum(q * k, axis=-1, keepdims=True) * scale
        q8 = jnp.broadcast_to(q, (V7X_SUBLANES, dk)).astype(BF16)
        o_cross = _dot(q8, state.astype(BF16))[0:1] * gamma
        o_ref[:, h * dv:(h + 1) * dv] = inner * v + o_cross
        k_diag = jnp.where(eye, jnp.broadcast_to(k * scale, (dk, dk)), 0.0).astype(BF16)
        v_rows = jnp.broadcast_to(v, (dk, dv)).astype(BF16)
        sn_ref[h] = gamma * state + _dot(k_diag, v_rows)


def retention_sample(q, k, v, state, layer, *, n_heads, dk, dv, log_gammas):
    nb = q.shape[0]
    wq, wv = n_heads * dk, n_heads * dv
    row = lambda w: pl.BlockSpec((None, 1, w), lambda b: (b, 0, 0))
    o, s_new = pl.pallas_call(
        functools.partial(_retention_step_kernel, n_heads=n_heads, dk=dk, dv=dv,
                          log_gammas=log_gammas),
        grid=(nb,),
        in_specs=[row(wq), row(wq), row(wv),
                  pl.BlockSpec((None, None, n_heads, dk, dv), lambda b: (layer, b, 0, 0, 0))],
        out_specs=(row(wv), pl.BlockSpec((None, n_heads, dk, dv), lambda b: (b, 0, 0, 0))),
        out_shape=(jax.ShapeDtypeStruct((nb, 1, wv), F32),
                   jax.ShapeDtypeStruct((nb, n_heads, dk, dv), F32)),
        compiler_params=_params("parallel"),
        name="retention_sample",
    )(q.reshape(nb, 1, wq), k.reshape(nb, 1, wq), v.reshape(nb, 1, wv), state)
    return o.reshape(nb, wv), s_new


def kernel(x_prompt, x_sample, mem_prompt, cache_k_diff, cache_v_diff, cache_mem_k, cache_mem_v,
           state_ret, page_table, norm_mix, w_in, w_out, da_lq1, da_lk1, da_lq2, da_lk2, da_subln,
           ret_norm, norm_x, norm_mem, wx_q, wx_kv, wx_o, norm_ff, w_up, w_down, norm_final):
    batch, seq, d_model = x_prompt.shape
    dec_batch, dec_seq, _ = x_sample.shape
    assert dec_seq == 1
    depth, n_pool, page, h_da, _, dh_da = cache_k_diff.shape
    dv_da = cache_v_diff.shape[-1]
    _, _, h_ret, dk_ret, dv_ret = state_ret.shape
    _, _, n_mem, h_x, dh_x = cache_mem_k.shape
    d_qk = h_da * 2 * dh_da
    d_da = h_da * dv_da
    d_rqk = h_ret * dk_ret
    d_ret = h_ret * dv_ret
    d_x = h_x * dh_x
    chunk = 128
    mp = batch * seq

    slopes = 2.0 ** (-8.0 * (jnp.arange(h_da, dtype=F32) + 1.0) / h_da)
    log_gammas = tuple(math.log(1.0 - 2.0 ** (-5.0 - h)) for h in range(h_ret))

    o_q, o_k, o_v = 0, d_qk, 2 * d_qk
    o_qr = o_v + d_da
    o_kr, o_vr = o_qr + d_rqk, o_qr + 2 * d_rqk
    o_g = o_vr + d_ret

    cache_kt = jnp.transpose(cache_k_diff, (0, 1, 3, 4, 5, 2)).reshape(depth, n_pool, d_qk, page)
    mem_tiles = n_mem * h_x // V7X_SUBLANES
    mem_k = cache_mem_k.reshape(depth, dec_batch, mem_tiles, V7X_SUBLANES, dh_x)
    mem_v = cache_mem_v.reshape(depth, dec_batch, mem_tiles, V7X_SUBLANES, dh_x)

    xp = x_prompt.reshape(mp, d_model)
    xs = x_sample.reshape(dec_batch, d_model)
    mem = mem_prompt.reshape(batch * n_mem, d_model)

    outs = {name: [] for name in ("pk", "pv", "ps", "pmk", "pmv", "sk", "sv", "ss")}
    for l in range(depth):
        lam_init = 0.8 - 0.6 * math.exp(-0.3 * l)
        lparams = tuple(a[l].reshape(1, dh_da) for a in (da_lq1, da_lk1, da_lq2, da_lk2))
        subln = da_subln[l].reshape(1, dv_da)
        w_in_l = w_in[l]
        w_a = jnp.concatenate([w_in_l[:, o_q:o_k], w_in_l[:, o_qr:o_g]], axis=1).astype(BF16)
        w_kv = w_in_l[:, o_k:o_qr].astype(BF16)
        w_g = w_in_l[:, o_g:].astype(BF16)
        w_out_da = w_out[l, :d_da].astype(BF16)
        w_out_r = w_out[l, d_da:].astype(BF16)
        wq_b, wkv_b, wo_b = wx_q[l].astype(BF16), wx_kv[l].astype(BF16), wx_o[l].astype(BF16)
        wu_b, wd_b = w_up[l].astype(BF16), w_down[l].astype(BF16)
        a_q, a_qr, a_kr, a_vr = 0, d_qk, d_qk + d_rqk, d_qk + 2 * d_rqk

        h = rmsnorm(xp, norm_mix[l], BF16)
        (pa,) = matmul([(h, w_a)], (BF16,))
        kv_f32, kv_b = matmul([(h, w_kv)], (F32, BF16))
        (g_r,) = matmul([(h, w_g)], (F32,))
        y_da = diff_attn_prompt(pa, kv_b, slopes, lparams, subln,
                                batch=batch, seq=seq, n_heads=h_da, dh=dh_da, dv=dv_da, lam_init=lam_init)
        y_r, s_p = retention_prompt(pa, g_r, ret_norm[l], batch=batch, seq=seq, n_heads=h_ret,
                                    dk=dk_ret, dv=dv_ret, chunk=chunk, log_gammas=log_gammas,
                                    q_col=a_qr, k_col=a_kr, v_col=a_vr)
        (xp,) = matmul([(y_da, w_out_da), (y_r, w_out_r)], (F32,), residual=xp)
        hm = rmsnorm(mem, norm_mem[l], BF16)
        mkv_f32, mkv_b = matmul([(hm, wkv_b)], (F32, BF16))
        hx = rmsnorm(xp, norm_x[l], BF16)
        (qx,) = matmul([(hx, wq_b)], (BF16,))
        ox = cross_attn_prompt(qx, mkv_b, batch=batch, seq=seq, n_mem=n_mem, n_heads=h_x, dh=dh_x)
        (xp,) = matmul([(ox, wo_b)], (F32,), residual=xp)
        xp = mlp(xp, norm_ff[l], wu_b, wd_b)
        outs["pk"].append(kv_f32[:, :d_qk].reshape(batch, seq, h_da, 2, dh_da))
        outs["pv"].append(kv_f32[:, d_qk:].reshape(batch, seq, h_da, dv_da))
        outs["ps"].append(s_p)
        outs["pmk"].append(mkv_f32[:, :d_x].reshape(batch, n_mem, h_x, dh_x))
        outs["pmv"].append(mkv_f32[:, d_x:].reshape(batch, n_mem, h_x, dh_x))

        h = rmsnorm(xs, norm_mix[l], BF16)
        (ps_a,) = matmul([(h, w_a)], (F32,))
        (kv_s,) = matmul([(h, w_kv)], (F32,))
        (g_s,) = matmul([(h, w_g)], (F32,))
        k_new, v_new = kv_s[:, :d_qk], kv_s[:, d_qk:]
        y_da = diff_attn_sample(ps_a[:, a_q:a_qr], k_new, v_new, cache_kt, cache_v_diff, l, page_table,
                                slopes, lparams, subln, n_heads=h_da, dh=dh_da, dv=dv_da,
                                lam_init=lam_init)
        o_r, s_s = retention_sample(ps_a[:, a_qr:a_kr], ps_a[:, a_kr:a_vr], ps_a[:, a_vr:], state_ret, l,
                                    n_heads=h_ret, dk=dk_ret, dv=dv_ret, log_gammas=log_gammas)
        y_r = gated_headnorm(o_r.reshape(dec_batch * h_ret, dv_ret), ret_norm[l],
                             g_s.reshape(dec_batch * h_ret, dv_ret))
        (xs,) = matmul([(y_da, w_out_da), (y_r.reshape(dec_batch, d_ret), w_out_r)],
                       (F32,), residual=xs)
        hx = rmsnorm(xs, norm_x[l], BF16)
        (qx,) = matmul([(hx, wq_b)], (F32,))
        ox = cross_attn_sample(qx, mem_k, mem_v, l, n_heads=h_x, dh=dh_x)
        (xs,) = matmul([(ox.astype(BF16), wo_b)], (F32,), residual=xs)
        xs = mlp(xs, norm_ff[l], wu_b, wd_b)
        outs["sk"].append(k_new.reshape(dec_batch, 1, h_da, 2, dh_da))
        outs["sv"].append(v_new.reshape(dec_batch, 1, h_da, dv_da))
        outs["ss"].append(s_s)

    y_prompt = rmsnorm(xp, norm_final, F32).reshape(batch, seq, d_model)
    y_sample = rmsnorm(xs, norm_final, F32).reshape(dec_batch, 1, d_model)
    st = lambda name: jnp.stack(outs[name])
    return (y_prompt, y_sample, st("pk"), st("pv"), st("ps"), st("pmk"), st("pmv"),
            st("sk"), st("sv"), st("ss"))
```

```python
import functools
import math

import jax
import jax.numpy as jnp
from jax import lax
from jax.experimental import pallas as pl
from jax.experimental.pallas import tpu as pltpu

EPS = 1e-5
NEG_INF = -1e30
LOG2E = math.log2(math.e)
F32 = jnp.float32
BF16 = jnp.bfloat16

V7X_VMEM_LIMIT_BYTES = 60 * 1024 * 1024
V7X_LANES = 128
V7X_SUBLANES = 8


def _params(*semantics):
    return pltpu.CompilerParams(dimension_semantics=semantics,
                                vmem_limit_bytes=V7X_VMEM_LIMIT_BYTES)


def _pick(n, candidates):
    for c in candidates:
        if n % c == 0:
            return c
    return n


def _dot(a, b):
    return jnp.dot(a, b, preferred_element_type=F32)


def _dot_nt(a, b):
    return lax.dot_general(a, b, (((1,), (1,)), ((), ())), preferred_element_type=F32)


def _rms(x, g):
    ms = jnp.mean(x * x, axis=-1, keepdims=True)
    return x * lax.rsqrt(ms + EPS) * g


def _rmsnorm_kernel(x_ref, g_ref, o_ref):
    o_ref[...] = _rms(x_ref[...], g_ref[...]).astype(o_ref.dtype)


def rmsnorm(x, g, out_dtype):
    m, d = x.shape
    tm = _pick(m, (512, 256, 128))
    return pl.pallas_call(
        _rmsnorm_kernel,
        grid=(m // tm,),
        in_specs=[pl.BlockSpec((tm, d), lambda i: (i, 0)),
                  pl.BlockSpec((1, d), lambda i: (0, 0))],
        out_specs=pl.BlockSpec((tm, d), lambda i: (i, 0)),
        out_shape=jax.ShapeDtypeStruct((m, d), out_dtype),
        compiler_params=_params("parallel"),
        name="rmsnorm",
    )(x, g.reshape(1, d))


def _gated_headnorm_kernel(x_ref, g_ref, gate_ref, o_ref):
    gate = gate_ref[...]
    y = _rms(x_ref[...], g_ref[...]) * (gate * jax.nn.sigmoid(gate))
    o_ref[...] = y.astype(o_ref.dtype)


def gated_headnorm(x, g, gate):
    r, d = x.shape
    tr = _pick(r, (512, 256, 128))
    blk = pl.BlockSpec((tr, d), lambda i: (i, 0))
    return pl.pallas_call(
        _gated_headnorm_kernel,
        grid=(r // tr,),
        in_specs=[blk, pl.BlockSpec((1, d), lambda i: (0, 0)), blk],
        out_specs=blk,
        out_shape=jax.ShapeDtypeStruct((r, d), BF16),
        compiler_params=_params("parallel"),
        name="gated_headnorm",
    )(x, g.reshape(1, d), gate)


def _mm_kernel(*refs, n_pairs, has_res):
    acc = _dot(refs[0][...], refs[1][...])
    for p in range(1, n_pairs):
        acc = acc + _dot(refs[2 * p][...], refs[2 * p + 1][...])
    pos = 2 * n_pairs
    if has_res:
        acc = refs[pos][...] + acc
        pos += 1
    for o_ref in refs[pos:]:
        o_ref[...] = acc.astype(o_ref.dtype)


def matmul(pairs, out_dtypes, residual=None):
    m = pairs[0][0].shape[0]
    n = pairs[0][1].shape[1]
    tm = _pick(m, (512, 256, 128))
    tn = _pick(n, (1024, 512, 256, 128))
    ins, specs = [], []
    for a, w in pairs:
        k = a.shape[1]
        ins += [a, w]
        specs += [pl.BlockSpec((tm, k), lambda j, i: (i, 0)),
                  pl.BlockSpec((k, tn), lambda j, i: (0, j))]
    if residual is not None:
        ins.append(residual)
        specs.append(pl.BlockSpec((tm, tn), lambda j, i: (i, j)))
    return pl.pallas_call(
        functools.partial(_mm_kernel, n_pairs=len(pairs), has_res=residual is not None),
        grid=(n // tn, m // tm),
        in_specs=specs,
        out_specs=tuple(pl.BlockSpec((tm, tn), lambda j, i: (i, j)) for _ in out_dtypes),
        out_shape=tuple(jax.ShapeDtypeStruct((m, n), dt) for dt in out_dtypes),
        compiler_params=_params("parallel", "parallel"),
        name="matmul",
    )(*ins)


def _vproj_kernel(a_ref, w_ref, *rest, n_heads):
    o4_ref, ob_ref = rest[-2:]
    acc = _dot(a_ref[...], w_ref[...])
    ob_ref[...] = acc.astype(BF16)
    dv = acc.shape[1] // n_heads
    for h in range(n_heads):
        o4_ref[:, h, :] = acc[:, h * dv:(h + 1) * dv]


def vproj(a, w, layer, depth, stacked, *, n_heads):
    m, k = a.shape
    n = w.shape[1]
    dv = n // n_heads
    tm = _pick(m, (512, 256, 128))
    ins = [a, w]
    in_specs = [pl.BlockSpec((tm, k), lambda i: (i, 0)), pl.BlockSpec((k, n), lambda i: (0, 0))]
    aliases = _stacked(in_specs, ins, stacked)
    return pl.pallas_call(
        functools.partial(_vproj_kernel, n_heads=n_heads),
        grid=(m // tm,),
        in_specs=in_specs,
        out_specs=(pl.BlockSpec((None, tm, n_heads, dv), lambda i: (layer, i, 0, 0)),
                   pl.BlockSpec((tm, n), lambda i: (i, 0))),
        out_shape=(jax.ShapeDtypeStruct((depth, m, n_heads, dv), F32),
                   jax.ShapeDtypeStruct((m, n), BF16)),
        input_output_aliases=aliases,
        compiler_params=_params("parallel"),
        name="vproj",
    )(*ins)


def _stacked(spec_list, ins, stacked, n_out_before=0):
    if stacked is None:
        return {}
    ins.append(stacked)
    spec_list.append(pl.BlockSpec(memory_space=pl.ANY))
    return {len(ins) - 1: n_out_before}


def _mlp_kernel(x_ref, g_ref, wu_ref, wd_ref, gn_ref, *refs, keep_x):
    acc_ref, h_ref = refs[0], refs[-1]
    f = pl.program_id(1)

    @pl.when(f == 0)
    def _():
        h_ref[...] = _rms(x_ref[...], g_ref[...]).astype(BF16)

    u = _dot(h_ref[...], wu_ref[...])
    a = jnp.square(jnp.maximum(u, 0.0)).astype(BF16)
    c = _dot(a, wd_ref[...])

    @pl.when(f == 0)
    def _():
        acc_ref[...] = x_ref[...] + c

    @pl.when(f != 0)
    def _():
        acc_ref[...] += c

    @pl.when(f == pl.num_programs(1) - 1)
    def _():
        n_ref = refs[1] if keep_x else acc_ref
        n_ref[...] = _rms(acc_ref[...], gn_ref[...]).astype(n_ref.dtype)


def mlp(x, g, w_up, w_down, g_next, *, keep_x):
    m, d = x.shape
    ff = w_up.shape[1]
    tm = _pick(m, (512, 256, 128))
    tf = _pick(ff, (2048, 1024, 512, 256, 128))
    row_blk = pl.BlockSpec((tm, d), lambda i, f: (i, 0))
    vec = pl.BlockSpec((1, d), lambda i, f: (0, 0))
    out_shape = [jax.ShapeDtypeStruct((m, d), F32)]
    if keep_x:
        out_shape.append(jax.ShapeDtypeStruct((m, d), BF16))
    return pl.pallas_call(
        functools.partial(_mlp_kernel, keep_x=keep_x),
        grid=(m // tm, ff // tf),
        in_specs=[row_blk, vec,
                  pl.BlockSpec((d, tf), lambda i, f: (0, f)),
                  pl.BlockSpec((tf, d), lambda i, f: (f, 0)),
                  vec],
        out_specs=tuple(row_blk for _ in out_shape),
        out_shape=tuple(out_shape),
        scratch_shapes=[pltpu.VMEM((tm, d), BF16)],
        compiler_params=_params("parallel", "arbitrary"),
        name="mlp",
    )(x, g.reshape(1, d), w_up, w_down, g_next.reshape(1, d))


def _diff_lambda(lq1_ref, lk1_ref, lq2_ref, lk2_ref, lam_init):
    e1 = jnp.exp(jnp.sum(lq1_ref[...] * lk1_ref[...], axis=-1, keepdims=True))
    e2 = jnp.exp(jnp.sum(lq2_ref[...] * lk2_ref[...], axis=-1, keepdims=True))
    return e1 - e2 + lam_init


def _lane_fold(x, op):
    out = x[:, :V7X_LANES]
    for c in range(V7X_LANES, x.shape[1], V7X_LANES):
        out = op(out, x[:, c:c + V7X_LANES])
    return out


def _diff_attn_kernel(slopes_ref, q_ref, k_ref, v_ref, lq1_ref, lk1_ref, lq2_ref, lk2_ref,
                      subln_ref, o_ref, s_ref, m_ref, l_ref, acc_ref, *, t, dh, lam_init):
    h = pl.program_id(1)
    i = pl.program_id(2)
    slope2 = slopes_ref[h] * LOG2E
    kpos0 = lax.broadcasted_iota(jnp.int32, (1, t), 1).astype(F32)

    q = q_ref[...].astype(F32) * (dh ** -0.5 * LOG2E)
    lane = lax.broadcasted_iota(jnp.int32, q.shape, 1)
    qq = jnp.concatenate([jnp.where(lane < dh, q, 0.0), jnp.where(lane >= dh, q, 0.0)],
                         axis=0).astype(BF16)

    def scores(j):
        start = pl.multiple_of(j * t, t)
        kpos = kpos0 + (j * t).astype(F32)
        return _dot_nt(qq, k_ref[pl.ds(start, t), :]) + slope2 * kpos

    def pass1(j, carry):
        s = scores(j)
        s_ref[:, pl.ds(pl.multiple_of(j * t, t), t)] = s
        m_ref[...] = jnp.maximum(m_ref[...], _lane_fold(s, jnp.maximum))
        return carry

    row = lax.broadcasted_iota(jnp.int32, (2 * t, t), 0)
    col = lax.broadcasted_iota(jnp.int32, (2 * t, t), 1)
    visible = jnp.where(row >= t, row - t, row) >= col
    s = jnp.where(visible, scores(i), NEG_INF)
    s_ref[:, pl.ds(pl.multiple_of(i * t, t), t)] = s
    m_ref[...] = _lane_fold(s, jnp.maximum)
    lax.fori_loop(0, i, pass1, 0)

    m = jnp.max(m_ref[...], axis=-1, keepdims=True)
    m_ref[...] = jnp.broadcast_to(m, m_ref.shape)
    l_ref[...] = jnp.zeros_like(l_ref)
    acc_ref[...] = jnp.zeros_like(acc_ref)

    def pass2(j, carry):
        start = pl.multiple_of(j * t, t)
        mb = m_ref[...]
        ps = []
        lsum = l_ref[...]
        for c in range(0, t, V7X_LANES):
            p = jnp.exp2(s_ref[:, pl.ds(start + c, V7X_LANES)] - mb)
            lsum = lsum + p
            ps.append(p.astype(BF16))
        l_ref[...] = lsum
        acc_ref[...] += _dot(jnp.concatenate(ps, axis=1), v_ref[pl.ds(start, t), :])
        return carry

    lax.fori_loop(0, i + 1, pass2, 0)

    lam = _diff_lambda(lq1_ref, lk1_ref, lq2_ref, lk2_ref, lam_init)
    o = acc_ref[...] / jnp.sum(l_ref[...], axis=-1, keepdims=True)
    o = o[:t] - lam * o[t:]
    o_ref[...] = (_rms(o, subln_ref[...]) * (1.0 - lam_init)).astype(o_ref.dtype)


def diff_attn_prompt(q, k, v, slopes, lparams, subln, *, batch, seq, n_heads, dh, dv, lam_init):
    t = _pick(seq, (512, 256, 128))
    nq = seq // t
    vec = lambda n: pl.BlockSpec((1, n), lambda b, h, i: (0, 0))
    return pl.pallas_call(
        functools.partial(_diff_attn_kernel, t=t, dh=dh, lam_init=lam_init),
        grid=(batch, n_heads, nq),
        in_specs=[pl.BlockSpec(memory_space=pltpu.SMEM),
                  pl.BlockSpec((t, 2 * dh), lambda b, h, i: (b * nq + i, h)),
                  pl.BlockSpec((seq, 2 * dh), lambda b, h, i: (b, h)),
                  pl.BlockSpec((seq, dv), lambda b, h, i: (b, h)),
                  vec(dh), vec(dh), vec(dh), vec(dh), vec(dv)],
        out_specs=pl.BlockSpec((t, dv), lambda b, h, i: (b * nq + i, h)),
        out_shape=jax.ShapeDtypeStruct((batch * seq, n_heads * dv), BF16),
        scratch_shapes=[pltpu.VMEM((2 * t, seq), F32),
                        pltpu.VMEM((2 * t, V7X_LANES), F32),
                        pltpu.VMEM((2 * t, V7X_LANES), F32),
                        pltpu.VMEM((2 * t, dv), F32)],
        compiler_params=_params("parallel", "parallel", "arbitrary"),
        name="diff_attn_prompt",
    )(slopes, q, k, v, *lparams, subln)


def _retention_kernel(q_ref, k_ref, v_ref, g_ref, norm_ref, y_ref, s_ref, *, c, n_heads, dk, dv,
                      log_gammas):
    step = pl.program_id(1)

    @pl.when(step == 0)
    def _():
        s_ref[...] = jnp.zeros_like(s_ref)

    row = lax.broadcasted_iota(jnp.int32, (c, c), 0)
    col = lax.broadcasted_iota(jnp.int32, (c, c), 1)
    diff = (row - col).astype(F32)
    n_v = lax.broadcasted_iota(jnp.int32, (c, dv), 0).astype(F32)
    n_k = lax.broadcasted_iota(jnp.int32, (c, dk), 0).astype(F32)
    scale = dk ** -0.5
    for h in range(n_heads):
        lg = log_gammas[h]
        q = q_ref[:, h * dk:(h + 1) * dk]
        k = k_ref[:, h * dk:(h + 1) * dk]
        v = v_ref[:, h * dv:(h + 1) * dv]
        state = s_ref[h]
        dmask = jnp.where(diff >= 0, jnp.exp(lg * jnp.maximum(diff, 0.0)), 0.0) * scale
        inner = _dot_nt(q, k) * dmask
        o = _dot(inner.astype(BF16), v)
        o = o + _dot(q, state.astype(BF16)) * jnp.exp(lg * (n_v + 1.0))
        k_dec = (k.astype(F32) * (jnp.exp(lg * (c - 1.0 - n_k)) * scale)).T
        s_ref[h] = math.exp(lg * c) * state + _dot(k_dec.astype(BF16), v)
        gate = g_ref[:, h * dv:(h + 1) * dv]
        y = _rms(o, norm_ref[...]) * (gate * jax.nn.sigmoid(gate))
        y_ref[:, h * dv:(h + 1) * dv] = y.astype(y_ref.dtype)


def retention_prompt(pa, g_r, ret_norm, *, batch, seq, n_heads, dk, dv, chunk, log_gammas,
                     q_col, k_col, v_col):
    nc = seq // chunk
    wq = n_heads * dk
    wv = n_heads * dv
    return pl.pallas_call(
        functools.partial(_retention_kernel, c=chunk, n_heads=n_heads, dk=dk, dv=dv,
                          log_gammas=log_gammas),
        grid=(batch, nc),
        in_specs=[pl.BlockSpec((chunk, wq), lambda b, s: (b * nc + s, q_col // wq)),
                  pl.BlockSpec((chunk, wq), lambda b, s: (b * nc + s, k_col // wq)),
                  pl.BlockSpec((chunk, wv), lambda b, s: (b * nc + s, v_col // wv)),
                  pl.BlockSpec((chunk, wv), lambda b, s: (b * nc + s, 0)),
                  pl.BlockSpec((1, dv), lambda b, s: (0, 0))],
        out_specs=(pl.BlockSpec((chunk, wv), lambda b, s: (b * nc + s, 0)),
                   pl.BlockSpec((None, n_heads, dk, dv), lambda b, s: (b, 0, 0, 0))),
        out_shape=(jax.ShapeDtypeStruct((batch * seq, wv), BF16),
                   jax.ShapeDtypeStruct((batch, n_heads, dk, dv), F32)),
        compiler_params=_params("parallel", "arbitrary"),
        name="retention_prompt",
    )(pa, pa, pa, g_r, ret_norm.reshape(1, dv))


def _cross_block_kernel(x_ref, g_ref, wq_ref, k_ref, v_ref, wo_ref, o_ref, *, n_heads, dh):
    x = x_ref[...]
    q = _dot(_rms(x, g_ref[...]).astype(BF16), wq_ref[...]).astype(BF16)
    heads = []
    for h in range(n_heads):
        sl = slice(h * dh, (h + 1) * dh)
        s = _dot_nt(q[:, sl], k_ref[:, sl]) * (dh ** -0.5)
        m = jnp.max(s, axis=-1, keepdims=True)
        p = jnp.exp(s - m)
        p = p / jnp.sum(p, axis=-1, keepdims=True)
        heads.append(_dot(p.astype(BF16), v_ref[:, sl]).astype(BF16))
    o_ref[...] = x + _dot(jnp.concatenate(heads, axis=1), wo_ref[...])


def cross_block_prompt(x, g, wq, mkv, wo, *, batch, seq, n_mem, n_heads, dh):
    d = x.shape[1]
    w = n_heads * dh
    tq = _pick(seq, (512, 256, 128))
    nq = seq // tq
    row_blk = pl.BlockSpec((tq, d), lambda b, i: (b * nq + i, 0))
    return pl.pallas_call(
        functools.partial(_cross_block_kernel, n_heads=n_heads, dh=dh),
        grid=(batch, nq),
        in_specs=[row_blk,
                  pl.BlockSpec((1, d), lambda b, i: (0, 0)),
                  pl.BlockSpec((d, w), lambda b, i: (0, 0)),
                  pl.BlockSpec((n_mem, w), lambda b, i: (b, 0)),
                  pl.BlockSpec((n_mem, w), lambda b, i: (b, 1)),
                  pl.BlockSpec((w, d), lambda b, i: (0, 0))],
        out_specs=row_blk,
        out_shape=jax.ShapeDtypeStruct(x.shape, F32),
        compiler_params=_params("parallel", "parallel"),
        name="cross_block_prompt",
    )(x, g.reshape(1, d), wq, mkv, mkv, wo)


def _diff_decode_kernel(*refs, n_pages, page, n_heads, dh, lam_init):
    pos = 1
    qt_ref, knt_ref, vn_ref = refs[pos:pos + 3]; pos += 3
    kt_refs = refs[pos:pos + n_pages]; pos += n_pages
    v_refs = refs[pos:pos + n_pages]; pos += n_pages
    slopes_ref = refs[pos]; pos += 1
    lparams = refs[pos:pos + 4]; pos += 4
    subln_ref = refs[pos]; pos += 1
    o_ref = refs[pos]

    b = pl.program_id(0)
    nb = qt_ref.shape[1]
    onehot = (lax.broadcasted_iota(jnp.int32, (nb, V7X_LANES), 0) == b).astype(BF16)
    q_rep = _dot(qt_ref[...], onehot) * (dh ** -0.5)
    kn_rep = _dot(knt_ref[...], onehot)

    def map_sums(x):
        x3 = x.reshape(n_heads, 2 * dh, x.shape[-1])
        return x3[:, :dh, :].sum(axis=1), x3[:, dh:, :].sum(axis=1)

    past = n_pages * page
    tpos = lax.broadcasted_iota(jnp.int32, (n_heads, page), 1).astype(F32)
    nslope = -slopes_ref[...]
    s1, s2 = [], []
    for p in range(n_pages):
        a, c = map_sums(kt_refs[p][...] * q_rep)
        bias = nslope * ((past - p * page) - tpos)
        s1.append(a + bias)
        s2.append(c + bias)
    n1, n2 = map_sums(kn_rep * q_rep)

    def softmax_parts(s_list, s_new):
        m = s_list[0]
        for s in s_list[1:]:
            m = jnp.maximum(m, s)
        m = jnp.maximum(jnp.max(m, axis=-1, keepdims=True), s_new)
        e = [jnp.exp(s - m) for s in s_list]
        e_new = jnp.exp(s_new - m)
        l = e[0]
        for x in e[1:]:
            l = l + x
        l = jnp.sum(l, axis=-1, keepdims=True) + e_new
        return e, e_new, 1.0 / l

    e1, en1, a1 = softmax_parts(s1, n1)
    e2, en2, a2 = softmax_parts(s2, n2)
    a2 = a2 * _diff_lambda(*lparams, lam_init)

    tok = lax.broadcasted_iota(jnp.int32, (page, n_heads, V7X_LANES), 0)
    lane = lax.broadcasted_iota(jnp.int32, (page, n_heads, V7X_LANES), 2)
    diag = tok == lane
    ones = jnp.ones((V7X_LANES, V7X_LANES), BF16)
    o = (en1 * a1 - en2 * a2) * vn_ref[...]
    for p in range(n_pages):
        w = e1[p] * a1 - e2[p] * a2
        wd = jnp.where(diag, w[None], 0.0).reshape(page * n_heads, V7X_LANES).astype(BF16)
        w3 = _dot(wd, ones).reshape(page, n_heads, V7X_LANES)
        o = o + jnp.sum(w3 * v_refs[p][...], axis=0)
    o_ref[...] = (_rms(o, subln_ref[...]) * (1.0 - lam_init)).astype(o_ref.dtype)


def diff_attn_sample(q, k_new, v_new, cache_kt, cache_v, layer, page_table, slopes, lparams, subln, *,
                     n_heads, dh, dv, lam_init):
    nb, n_pages = page_table.shape
    page = cache_v.shape[2]
    wk = n_heads * 2 * dh
    assert page == V7X_LANES and dv == V7X_LANES and n_heads == V7X_SUBLANES
    const = lambda shape: pl.BlockSpec(shape, lambda b, pt: (0,) * len(shape))

    def kt_spec(p):
        return pl.BlockSpec((None, None, wk, page), lambda b, pt: (layer, pt[b, p], 0, 0))

    def v_spec(p):
        return pl.BlockSpec((None, None, page, n_heads, dv), lambda b, pt: (layer, pt[b, p], 0, 0, 0))

    in_specs = ([const((wk, nb)), const((wk, nb)),
                 pl.BlockSpec((None, n_heads, dv), lambda b, pt: (b, 0, 0))]
                + [kt_spec(p) for p in range(n_pages)] + [v_spec(p) for p in range(n_pages)]
                + [const((n_heads, V7X_LANES))] + [const((1, dh))] * 4 + [const((1, dv))])
    out = pl.pallas_call(
        functools.partial(_diff_decode_kernel, n_pages=n_pages, page=page, n_heads=n_heads, dh=dh,
                          lam_init=lam_init),
        grid_spec=pltpu.PrefetchScalarGridSpec(
            num_scalar_prefetch=1, grid=(nb,), in_specs=in_specs,
            out_specs=pl.BlockSpec((None, n_heads, dv), lambda b, pt: (b, 0, 0))),
        out_shape=jax.ShapeDtypeStruct((nb, n_heads, dv), BF16),
        compiler_params=_params("arbitrary"),
        name="diff_attn_sample",
    )(page_table, q.T.astype(BF16), k_new.T.astype(BF16), v_new.reshape(nb, n_heads, dv),
      *([cache_kt] * n_pages), *([cache_v] * n_pages),
      jnp.broadcast_to(slopes[:, None], (n_heads, V7X_LANES)), *lparams, subln)
    return out.reshape(nb, n_heads * dv)


def _cross_decode_kernel(q_ref, k_ref, v_ref, o_ref, *, n_heads, dh):
    n_tiles = k_ref.shape[0]
    q = q_ref[...] * (dh ** -0.5)
    ones = jnp.ones((dh, V7X_LANES), BF16)
    prod = (k_ref[...] * q[None]).reshape(n_tiles * V7X_SUBLANES, dh).astype(BF16)
    s = _dot(prod, ones).reshape(n_tiles, V7X_SUBLANES, V7X_LANES)

    def over_tokens(x, op):
        out = x
        for r in range(n_heads, V7X_SUBLANES, n_heads):
            out = op(out, pltpu.roll(x, r, axis=0))
        return out

    m = over_tokens(jnp.max(s, axis=0), jnp.maximum)
    e = jnp.exp(s - m[None])
    l = over_tokens(jnp.sum(e, axis=0), jnp.add)
    o = over_tokens(jnp.sum(e * v_ref[...], axis=0), jnp.add)
    o_ref[...] = o / l


def cross_attn_sample(q, cache_k, cache_v, layer, *, n_heads, dh):
    nb = q.shape[0]
    assert dh == V7X_LANES and V7X_SUBLANES % n_heads == 0
    n_tiles = cache_k.shape[2]
    q8 = jnp.tile(q.reshape(nb, n_heads, dh), (1, V7X_SUBLANES // n_heads, 1))
    kv_spec = pl.BlockSpec((None, None, n_tiles, V7X_SUBLANES, dh), lambda b: (layer, b, 0, 0, 0))
    row = pl.BlockSpec((None, V7X_SUBLANES, dh), lambda b: (b, 0, 0))
    out = pl.pallas_call(
        functools.partial(_cross_decode_kernel, n_heads=n_heads, dh=dh),
        grid=(nb,),
        in_specs=[row, kv_spec, kv_spec],
        out_specs=row,
        out_shape=jax.ShapeDtypeStruct((nb, V7X_SUBLANES, dh), F32),
        compiler_params=_params("parallel"),
        name="cross_attn_sample",
    )(q8, cache_k, cache_v)
    return out[:, :n_heads].reshape(nb, n_heads * dh)


def _retention_step_kernel(q_ref, k_ref, v_ref, s_ref, *rest, n_heads, dk, dv, log_gammas):
    o_ref, sn_ref = rest[-2:]
    scale = dk ** -0.5
    eye = (lax.broadcasted_iota(jnp.int32, (dk, dk), 0)
           == lax.broadcasted_iota(jnp.int32, (dk, dk), 1))
    for h in range(n_heads):
        gamma = math.exp(log_gammas[h])
        q = q_ref[:, h * dk:(h + 1) * dk]
        k = k_ref[:, h * dk:(h + 1) * dk]
        v = v_ref[:, h * dv:(h + 1) * dv]
        state = s_ref[h]
        inner = jnp.sum(q * k, axis=-1, keepdims=True) * scale
        q8 = jnp.broadcast_to(q, (V7X_SUBLANES, dk)).astype(BF16)
        o_cross = _dot(q8, state.astype(BF16))[0:1] * gamma
        o_ref[:, h * dv:(h + 1) * dv] = inner * v + o_cross
        k_diag = jnp.where(eye, jnp.broadcast_to(k * scale, (dk, dk)), 0.0).astype(BF16)
        v_rows = jnp.broadcast_to(v, (dk, dv)).astype(BF16)
        sn_ref[h] = gamma * state + _dot(k_diag, v_rows)


def retention_sample(q, k, v, state, layer, stacked, *, n_heads, dk, dv, log_gammas):
    nb = q.shape[0]
    wq, wv = n_heads * dk, n_heads * dv
    row = lambda w: pl.BlockSpec((None, 1, w), lambda b, *_: (b, 0, 0))
    state_blk = pl.BlockSpec((None, None, n_heads, dk, dv), lambda b, *_: (layer, b, 0, 0, 0))
    ins = [q.reshape(nb, 1, wq), k.reshape(nb, 1, wq), v.reshape(nb, 1, wv), state]
    in_specs = [row(wq), row(wq), row(wv), state_blk]
    aliases = _stacked(in_specs, ins, stacked, n_out_before=1)
    o, s_new = pl.pallas_call(
        functools.partial(_retention_step_kernel, n_heads=n_heads, dk=dk, dv=dv,
                          log_gammas=log_gammas),
        grid=(nb,),
        in_specs=in_specs,
        out_specs=(row(wv), state_blk),
        out_shape=(jax.ShapeDtypeStruct((nb, 1, wv), F32),
                   jax.ShapeDtypeStruct(state.shape, F32)),
        input_output_aliases=aliases,
        compiler_params=_params("parallel"),
        name="retention_sample",
    )(*ins)
    return o.reshape(nb, wv), s_new


def kernel(x_prompt, x_sample, mem_prompt, cache_k_diff, cache_v_diff, cache_mem_k, cache_mem_v,
           state_ret, page_table, norm_mix, w_in, w_out, da_lq1, da_lk1, da_lq2, da_lk2, da_subln,
           ret_norm, norm_x, norm_mem, wx_q, wx_kv, wx_o, norm_ff, w_up, w_down, norm_final):
    batch, seq, d_model = x_prompt.shape
    dec_batch, dec_seq, _ = x_sample.shape
    assert dec_seq == 1
    depth, n_pool, page, h_da, _, dh_da = cache_k_diff.shape
    dv_da = cache_v_diff.shape[-1]
    _, _, h_ret, dk_ret, dv_ret = state_ret.shape
    _, _, n_mem, h_x, dh_x = cache_mem_k.shape
    d_qk = h_da * 2 * dh_da
    d_da = h_da * dv_da
    d_rqk = h_ret * dk_ret
    d_ret = h_ret * dv_ret
    d_x = h_x * dh_x
    chunk = 128
    mp = batch * seq

    slopes = 2.0 ** (-8.0 * (jnp.arange(h_da, dtype=F32) + 1.0) / h_da)
    log_gammas = tuple(math.log(1.0 - 2.0 ** (-5.0 - h)) for h in range(h_ret))

    o_q, o_k, o_v = 0, d_qk, 2 * d_qk
    o_qr = o_v + d_da
    o_kr, o_vr = o_qr + d_rqk, o_qr + 2 * d_rqk
    o_g = o_vr + d_ret

    cache_kt = jnp.transpose(cache_k_diff, (0, 1, 3, 4, 5, 2)).reshape(depth, n_pool, d_qk, page)
    mem_tiles = n_mem * h_x // V7X_SUBLANES
    mem_k = cache_mem_k.reshape(depth, dec_batch, mem_tiles, V7X_SUBLANES, dh_x)
    mem_v = cache_mem_v.reshape(depth, dec_batch, mem_tiles, V7X_SUBLANES, dh_x)

    xp = x_prompt.reshape(mp, d_model)
    xs = x_sample.reshape(dec_batch, d_model)
    mem = mem_prompt.reshape(batch * n_mem, d_model)

    outs = {name: [] for name in ("pk", "ps", "pmk", "pmv", "sk", "sv")}
    pv_all = ss_all = None
    for l in range(depth):
        lam_init = 0.8 - 0.6 * math.exp(-0.3 * l)
        lparams = tuple(a[l].reshape(1, dh_da) for a in (da_lq1, da_lk1, da_lq2, da_lk2))
        subln = da_subln[l].reshape(1, dv_da)
        w_in_l = w_in[l]
        w_a = jnp.concatenate([w_in_l[:, o_q:o_k], w_in_l[:, o_qr:o_g]], axis=1).astype(BF16)
        w_k = w_in_l[:, o_k:o_v].astype(BF16)
        w_v = w_in_l[:, o_v:o_qr].astype(BF16)
        w_g = w_in_l[:, o_g:].astype(BF16)
        w_out_da = w_out[l, :d_da].astype(BF16)
        w_out_r = w_out[l, d_da:].astype(BF16)
        wq_b, wkv_b, wo_b = wx_q[l].astype(BF16), wx_kv[l].astype(BF16), wx_o[l].astype(BF16)
        wu_b, wd_b = w_up[l].astype(BF16), w_down[l].astype(BF16)
        a_q, a_qr, a_kr, a_vr = 0, d_qk, d_qk + d_rqk, d_qk + 2 * d_rqk

        last = l == depth - 1
        g_next = norm_final if last else norm_mix[l + 1]

        h = rmsnorm(xp, norm_mix[l], BF16) if l == 0 else hp_next
        (pa,) = matmul([(h, w_a)], (BF16,))
        k_f32, k_b = matmul([(h, w_k)], (F32, BF16))
        pv_all, v_b = vproj(h, w_v, l, depth, pv_all, n_heads=h_da)
        (g_r,) = matmul([(h, w_g)], (F32,))
        y_da = diff_attn_prompt(pa, k_b, v_b, slopes, lparams, subln,
                                batch=batch, seq=seq, n_heads=h_da, dh=dh_da, dv=dv_da, lam_init=lam_init)
        y_r, s_p = retention_prompt(pa, g_r, ret_norm[l], batch=batch, seq=seq, n_heads=h_ret,
                                    dk=dk_ret, dv=dv_ret, chunk=chunk, log_gammas=log_gammas,
                                    q_col=a_qr, k_col=a_kr, v_col=a_vr)
        (xp,) = matmul([(y_da, w_out_da), (y_r, w_out_r)], (F32,), residual=xp)
        hm = rmsnorm(mem, norm_mem[l], BF16)
        mkv_f32, mkv_b = matmul([(hm, wkv_b)], (F32, BF16))
        xp = cross_block_prompt(xp, norm_x[l], wq_b, mkv_b, wo_b, batch=batch, seq=seq, n_mem=n_mem,
                                n_heads=h_x, dh=dh_x)
        if last:
            (y_prompt,) = mlp(xp, norm_ff[l], wu_b, wd_b, g_next, keep_x=False)
        else:
            xp, hp_next = mlp(xp, norm_ff[l], wu_b, wd_b, g_next, keep_x=True)
        outs["pk"].append(k_f32.reshape(batch, seq, h_da, 2, dh_da))
        outs["ps"].append(s_p)
        outs["pmk"].append(mkv_f32[:, :d_x].reshape(batch, n_mem, h_x, dh_x))
        outs["pmv"].append(mkv_f32[:, d_x:].reshape(batch, n_mem, h_x, dh_x))

        h = rmsnorm(xs, norm_mix[l], BF16) if l == 0 else hs_next
        (ps_a,) = matmul([(h, w_a)], (F32,))
        (k_new,) = matmul([(h, w_k)], (F32,))
        (v_new,) = matmul([(h, w_v)], (F32,))
        (g_s,) = matmul([(h, w_g)], (F32,))
        y_da = diff_attn_sample(ps_a[:, a_q:a_qr], k_new, v_new, cache_kt, cache_v_diff, l, page_table,
                                slopes, lparams, subln, n_heads=h_da, dh=dh_da, dv=dv_da,
                                lam_init=lam_init)
        o_r, ss_all = retention_sample(ps_a[:, a_qr:a_kr], ps_a[:, a_kr:a_vr], ps_a[:, a_vr:], state_ret, l, ss_all,
                                    n_heads=h_ret, dk=dk_ret, dv=dv_ret, log_gammas=log_gammas)
        y_r = gated_headnorm(o_r.reshape(dec_batch * h_ret, dv_ret), ret_norm[l],
                             g_s.reshape(dec_batch * h_ret, dv_ret))
        (xs,) = matmul([(y_da, w_out_da), (y_r.reshape(dec_batch, d_ret), w_out_r)],
                       (F32,), residual=xs)
        hx = rmsnorm(xs, norm_x[l], BF16)
        (qx,) = matmul([(hx, wq_b)], (F32,))
        ox = cross_attn_sample(qx, mem_k, mem_v, l, n_heads=h_x, dh=dh_x)
        (xs,) = matmul([(ox.astype(BF16), wo_b)], (F32,), residual=xs)
        if last:
            (y_sample,) = mlp(xs, norm_ff[l], wu_b, wd_b, g_next, keep_x=False)
        else:
            xs, hs_next = mlp(xs, norm_ff[l], wu_b, wd_b, g_next, keep_x=True)
        outs["sk"].append(k_new.reshape(dec_batch, 1, h_da, 2, dh_da))
        outs["sv"].append(v_new.reshape(dec_batch, 1, h_da, dv_da))

    y_prompt = y_prompt.reshape(batch, seq, d_model)
    y_sample = y_sample.reshape(dec_batch, 1, d_model)
    st = lambda name: jnp.stack(outs[name])
    return (y_prompt, y_sample, st("pk"), pv_all.reshape(depth, batch, seq, h_da, dv_da), st("ps"),
            st("pmk"), st("pmv"), st("sk"), st("sv"), ss_all)
```

```python
import functools
import math

import jax
import jax.numpy as jnp
from jax import lax
from jax.experimental import pallas as pl
from jax.experimental.pallas import tpu as pltpu

EPS = 1e-5
NEG_INF = -1e30
LOG2E = math.log2(math.e)
F32 = jnp.float32
BF16 = jnp.bfloat16

V7X_VMEM_LIMIT_BYTES = 60 * 1024 * 1024
V7X_LANES = 128
V7X_SUBLANES = 8


def _params(*semantics):
    return pltpu.CompilerParams(dimension_semantics=semantics,
                                vmem_limit_bytes=V7X_VMEM_LIMIT_BYTES)


def _pick(n, candidates):
    for c in candidates:
        if n % c == 0:
            return c
    return n


def _dot(a, b):
    return jnp.dot(a, b, preferred_element_type=F32)


def _dot_nt(a, b):
    return lax.dot_general(a, b, (((1,), (1,)), ((), ())), preferred_element_type=F32)


def _rms(x, g):
    ms = jnp.mean(x * x, axis=-1, keepdims=True)
    return x * lax.rsqrt(ms + EPS) * g


def _rmsnorm_kernel(x_ref, g_ref, o_ref):
    o_ref[...] = _rms(x_ref[...], g_ref[...]).astype(o_ref.dtype)


def rmsnorm(x, g, out_dtype):
    m, d = x.shape
    tm = _pick(m, (512, 256, 128))
    return pl.pallas_call(
        _rmsnorm_kernel,
        grid=(m // tm,),
        in_specs=[pl.BlockSpec((tm, d), lambda i: (i, 0)),
                  pl.BlockSpec((1, d), lambda i: (0, 0))],
        out_specs=pl.BlockSpec((tm, d), lambda i: (i, 0)),
        out_shape=jax.ShapeDtypeStruct((m, d), out_dtype),
        compiler_params=_params("parallel"),
        name="rmsnorm",
    )(x, g.reshape(1, d))


def _gated_headnorm_kernel(x_ref, g_ref, gate_ref, o_ref):
    gate = gate_ref[...]
    y = _rms(x_ref[...], g_ref[...]) * (gate * jax.nn.sigmoid(gate))
    o_ref[...] = y.astype(o_ref.dtype)


def gated_headnorm(x, g, gate):
    r, d = x.shape
    tr = _pick(r, (512, 256, 128))
    blk = pl.BlockSpec((tr, d), lambda i: (i, 0))
    return pl.pallas_call(
        _gated_headnorm_kernel,
        grid=(r // tr,),
        in_specs=[blk, pl.BlockSpec((1, d), lambda i: (0, 0)), blk],
        out_specs=blk,
        out_shape=jax.ShapeDtypeStruct((r, d), BF16),
        compiler_params=_params("parallel"),
        name="gated_headnorm",
    )(x, g.reshape(1, d), gate)


def _mm_kernel(*refs, n_pairs, has_res):
    acc = _dot(refs[0][...], refs[1][...])
    for p in range(1, n_pairs):
        acc = acc + _dot(refs[2 * p][...], refs[2 * p + 1][...])
    pos = 2 * n_pairs
    if has_res:
        acc = refs[pos][...] + acc
        pos += 1
    for o_ref in refs[pos:]:
        o_ref[...] = acc.astype(o_ref.dtype)


def matmul(pairs, out_dtypes, residual=None):
    m = pairs[0][0].shape[0]
    n = pairs[0][1].shape[1]
    tm = _pick(m, (512, 256, 128))
    tn = _pick(n, (1024, 512, 256, 128))
    ins, specs = [], []
    for a, w in pairs:
        k = a.shape[1]
        ins += [a, w]
        specs += [pl.BlockSpec((tm, k), lambda j, i: (i, 0)),
                  pl.BlockSpec((k, tn), lambda j, i: (0, j))]
    if residual is not None:
        ins.append(residual)
        specs.append(pl.BlockSpec((tm, tn), lambda j, i: (i, j)))
    return pl.pallas_call(
        functools.partial(_mm_kernel, n_pairs=len(pairs), has_res=residual is not None),
        grid=(n // tn, m // tm),
        in_specs=specs,
        out_specs=tuple(pl.BlockSpec((tm, tn), lambda j, i: (i, j)) for _ in out_dtypes),
        out_shape=tuple(jax.ShapeDtypeStruct((m, n), dt) for dt in out_dtypes),
        compiler_params=_params("parallel", "parallel"),
        name="matmul",
    )(*ins)


def _vproj_kernel(a_ref, w_ref, *rest, n_heads):
    o4_ref, ob_ref = rest[-2:]
    acc = _dot(a_ref[...], w_ref[...])
    ob_ref[...] = acc.astype(BF16)
    dv = acc.shape[1] // n_heads
    for h in range(n_heads):
        o4_ref[:, h, :] = acc[:, h * dv:(h + 1) * dv]


def vproj(a, w, layer, depth, stacked, *, n_heads):
    m, k = a.shape
    n = w.shape[1]
    dv = n // n_heads
    tm = _pick(m, (512, 256, 128))
    ins = [a, w]
    in_specs = [pl.BlockSpec((tm, k), lambda i: (i, 0)), pl.BlockSpec((k, n), lambda i: (0, 0))]
    aliases = _stacked(in_specs, ins, stacked)
    return pl.pallas_call(
        functools.partial(_vproj_kernel, n_heads=n_heads),
        grid=(m // tm,),
        in_specs=in_specs,
        out_specs=(pl.BlockSpec((None, tm, n_heads, dv), lambda i: (layer, i, 0, 0)),
                   pl.BlockSpec((tm, n), lambda i: (i, 0))),
        out_shape=(jax.ShapeDtypeStruct((depth, m, n_heads, dv), F32),
                   jax.ShapeDtypeStruct((m, n), BF16)),
        input_output_aliases=aliases,
        compiler_params=_params("parallel"),
        name="vproj",
    )(*ins)


def _stacked(spec_list, ins, stacked, n_out_before=0):
    if stacked is None:
        return {}
    ins.append(stacked)
    spec_list.append(pl.BlockSpec(memory_space=pl.ANY))
    return {len(ins) - 1: n_out_before}


def _mlp_kernel(x_ref, g_ref, wu_ref, wd_ref, gn_ref, *refs, keep_x):
    acc_ref, h_ref = refs[0], refs[-1]
    f = pl.program_id(1)

    @pl.when(f == 0)
    def _():
        h_ref[...] = _rms(x_ref[...], g_ref[...]).astype(BF16)

    u = _dot(h_ref[...], wu_ref[...])
    a = jnp.square(jnp.maximum(u, 0.0)).astype(BF16)
    c = _dot(a, wd_ref[...])

    @pl.when(f == 0)
    def _():
        acc_ref[...] = x_ref[...] + c

    @pl.when(f != 0)
    def _():
        acc_ref[...] += c

    @pl.when(f == pl.num_programs(1) - 1)
    def _():
        n_ref = refs[1] if keep_x else acc_ref
        n_ref[...] = _rms(acc_ref[...], gn_ref[...]).astype(n_ref.dtype)


def mlp(x, g, w_up, w_down, g_next, *, keep_x):
    m, d = x.shape
    ff = w_up.shape[1]
    tm = _pick(m, (512, 256, 128))
    tf = _pick(ff, (2048, 1024, 512, 256, 128))
    row_blk = pl.BlockSpec((tm, d), lambda i, f: (i, 0))
    vec = pl.BlockSpec((1, d), lambda i, f: (0, 0))
    out_shape = [jax.ShapeDtypeStruct((m, d), F32)]
    if keep_x:
        out_shape.append(jax.ShapeDtypeStruct((m, d), BF16))
    return pl.pallas_call(
        functools.partial(_mlp_kernel, keep_x=keep_x),
        grid=(m // tm, ff // tf),
        in_specs=[row_blk, vec,
                  pl.BlockSpec((d, tf), lambda i, f: (0, f)),
                  pl.BlockSpec((tf, d), lambda i, f: (f, 0)),
                  vec],
        out_specs=tuple(row_blk for _ in out_shape),
        out_shape=tuple(out_shape),
        scratch_shapes=[pltpu.VMEM((tm, d), BF16)],
        compiler_params=_params("parallel", "arbitrary"),
        name="mlp",
    )(x, g.reshape(1, d), w_up, w_down, g_next.reshape(1, d))


def _diff_lambda(lq1_ref, lk1_ref, lq2_ref, lk2_ref, lam_init):
    e1 = jnp.exp(jnp.sum(lq1_ref[...] * lk1_ref[...], axis=-1, keepdims=True))
    e2 = jnp.exp(jnp.sum(lq2_ref[...] * lk2_ref[...], axis=-1, keepdims=True))
    return e1 - e2 + lam_init


def _lane_fold(x, op):
    out = x[:, :V7X_LANES]
    for c in range(V7X_LANES, x.shape[1], V7X_LANES):
        out = op(out, x[:, c:c + V7X_LANES])
    return out


def _diff_attn_kernel(slopes_ref, q_ref, k_ref, v_ref, lq1_ref, lk1_ref, lq2_ref, lk2_ref,
                      subln_ref, o_ref, s_ref, m_ref, l_ref, acc_ref, *, t, dh, dv, group, lam_init):
    hg = pl.program_id(1)
    i = pl.program_id(2)
    kpos0 = lax.broadcasted_iota(jnp.int32, (1, t), 1).astype(F32)
    lane = lax.broadcasted_iota(jnp.int32, (t, 2 * dh), 1)

    qq, slope2 = [], []
    for g in range(group):
        q = q_ref[:, g * 2 * dh:(g + 1) * 2 * dh].astype(F32) * (dh ** -0.5 * LOG2E)
        qq.append(jnp.concatenate([jnp.where(lane < dh, q, 0.0), jnp.where(lane >= dh, q, 0.0)],
                                  axis=0).astype(BF16))
        slope2.append(slopes_ref[hg * group + g] * LOG2E)

    def scores(g, j):
        start = pl.multiple_of(j * t, t)
        kpos = kpos0 + (j * t).astype(F32)
        return _dot_nt(qq[g], k_ref[pl.ds(start, t), g * 2 * dh:(g + 1) * 2 * dh]) + slope2[g] * kpos

    def pass1(j, carry):
        for g in range(group):
            s = scores(g, j)
            s_ref[g, :, pl.ds(pl.multiple_of(j * t, t), t)] = s
            m_ref[g] = jnp.maximum(m_ref[g], _lane_fold(s, jnp.maximum))
        return carry

    row = lax.broadcasted_iota(jnp.int32, (2 * t, t), 0)
    col = lax.broadcasted_iota(jnp.int32, (2 * t, t), 1)
    visible = jnp.where(row >= t, row - t, row) >= col
    for g in range(group):
        s = jnp.where(visible, scores(g, i), NEG_INF)
        s_ref[g, :, pl.ds(pl.multiple_of(i * t, t), t)] = s
        m_ref[g] = _lane_fold(s, jnp.maximum)
    lax.fori_loop(0, i, pass1, 0)

    for g in range(group):
        m = jnp.max(m_ref[g], axis=-1, keepdims=True)
        m_ref[g] = jnp.broadcast_to(m, (2 * t, V7X_LANES))
    l_ref[...] = jnp.zeros_like(l_ref)
    acc_ref[...] = jnp.zeros_like(acc_ref)

    def pass2(j, carry):
        start = pl.multiple_of(j * t, t)
        for g in range(group):
            mb = m_ref[g]
            ps = []
            lsum = l_ref[g]
            for c in range(0, t, V7X_LANES):
                p = jnp.exp2(s_ref[g, :, pl.ds(start + c, V7X_LANES)] - mb)
                lsum = lsum + p
                ps.append(p.astype(BF16))
            l_ref[g] = lsum
            acc_ref[g] += _dot(jnp.concatenate(ps, axis=1), v_ref[pl.ds(start, t), g * dv:(g + 1) * dv])
        return carry

    lax.fori_loop(0, i + 1, pass2, 0)

    lam = _diff_lambda(lq1_ref, lk1_ref, lq2_ref, lk2_ref, lam_init)
    for g in range(group):
        o = acc_ref[g] / jnp.sum(l_ref[g], axis=-1, keepdims=True)
        o = o[:t] - lam * o[t:]
        o_ref[:, g * dv:(g + 1) * dv] = (_rms(o, subln_ref[...]) * (1.0 - lam_init)).astype(o_ref.dtype)


def diff_attn_prompt(q, k, v, slopes, lparams, subln, *, batch, seq, n_heads, dh, dv, lam_init):
    t = _pick(seq, (512, 256, 128))
    nq = seq // t
    group = _pick(n_heads, (4, 2, 1))
    vec = lambda n: pl.BlockSpec((1, n), lambda b, h, i: (0, 0))
    return pl.pallas_call(
        functools.partial(_diff_attn_kernel, t=t, dh=dh, dv=dv, group=group, lam_init=lam_init),
        grid=(batch, n_heads // group, nq),
        in_specs=[pl.BlockSpec(memory_space=pltpu.SMEM),
                  pl.BlockSpec((t, group * 2 * dh), lambda b, h, i: (b * nq + i, h)),
                  pl.BlockSpec((seq, group * 2 * dh), lambda b, h, i: (b, h)),
                  pl.BlockSpec((seq, group * dv), lambda b, h, i: (b, h)),
                  vec(dh), vec(dh), vec(dh), vec(dh), vec(dv)],
        out_specs=pl.BlockSpec((t, group * dv), lambda b, h, i: (b * nq + i, h)),
        out_shape=jax.ShapeDtypeStruct((batch * seq, n_heads * dv), BF16),
        scratch_shapes=[pltpu.VMEM((group, 2 * t, seq), F32),
                        pltpu.VMEM((group, 2 * t, V7X_LANES), F32),
                        pltpu.VMEM((group, 2 * t, V7X_LANES), F32),
                        pltpu.VMEM((group, 2 * t, dv), F32)],
        compiler_params=_params("parallel", "parallel", "arbitrary"),
        name="diff_attn_prompt",
    )(slopes, q, k, v, *lparams, subln)


def _retention_kernel(q_ref, k_ref, v_ref, g_ref, norm_ref, y_ref, s_ref, *, c, n_heads, dk, dv,
                      log_gammas):
    step = pl.program_id(1)

    @pl.when(step == 0)
    def _():
        s_ref[...] = jnp.zeros_like(s_ref)

    row = lax.broadcasted_iota(jnp.int32, (c, c), 0)
    col = lax.broadcasted_iota(jnp.int32, (c, c), 1)
    diff = (row - col).astype(F32)
    n_v = lax.broadcasted_iota(jnp.int32, (c, dv), 0).astype(F32)
    n_k = lax.broadcasted_iota(jnp.int32, (c, dk), 0).astype(F32)
    scale = dk ** -0.5
    for h in range(n_heads):
        lg = log_gammas[h]
        q = q_ref[:, h * dk:(h + 1) * dk]
        k = k_ref[:, h * dk:(h + 1) * dk]
        v = v_ref[:, h * dv:(h + 1) * dv]
        state = s_ref[h]
        dmask = jnp.where(diff >= 0, jnp.exp(lg * jnp.maximum(diff, 0.0)), 0.0) * scale
        inner = _dot_nt(q, k) * dmask
        o = _dot(inner.astype(BF16), v)
        o = o + _dot(q, state.astype(BF16)) * jnp.exp(lg * (n_v + 1.0))
        k_dec = (k.astype(F32) * (jnp.exp(lg * (c - 1.0 - n_k)) * scale)).T
        s_ref[h] = math.exp(lg * c) * state + _dot(k_dec.astype(BF16), v)
        gate = g_ref[:, h * dv:(h + 1) * dv]
        y = _rms(o, norm_ref[...]) * (gate * jax.nn.sigmoid(gate))
        y_ref[:, h * dv:(h + 1) * dv] = y.astype(y_ref.dtype)


def retention_prompt(pa, g_r, ret_norm, *, batch, seq, n_heads, dk, dv, chunk, log_gammas,
                     q_col, k_col, v_col):
    nc = seq // chunk
    wq = n_heads * dk
    wv = n_heads * dv
    return pl.pallas_call(
        functools.partial(_retention_kernel, c=chunk, n_heads=n_heads, dk=dk, dv=dv,
                          log_gammas=log_gammas),
        grid=(batch, nc),
        in_specs=[pl.BlockSpec((chunk, wq), lambda b, s: (b * nc + s, q_col // wq)),
                  pl.BlockSpec((chunk, wq), lambda b, s: (b * nc + s, k_col // wq)),
                  pl.BlockSpec((chunk, wv), lambda b, s: (b * nc + s, v_col // wv)),
                  pl.BlockSpec((chunk, wv), lambda b, s: (b * nc + s, 0)),
                  pl.BlockSpec((1, dv), lambda b, s: (0, 0))],
        out_specs=(pl.BlockSpec((chunk, wv), lambda b, s: (b * nc + s, 0)),
                   pl.BlockSpec((None, n_heads, dk, dv), lambda b, s: (b, 0, 0, 0))),
        out_shape=(jax.ShapeDtypeStruct((batch * seq, wv), BF16),
                   jax.ShapeDtypeStruct((batch, n_heads, dk, dv), F32)),
        compiler_params=_params("parallel", "arbitrary"),
        name="retention_prompt",
    )(pa, pa, pa, g_r, ret_norm.reshape(1, dv))


def _cross_block_kernel(x_ref, g_ref, wq_ref, k_ref, v_ref, wo_ref, o_ref, *, n_heads, dh):
    x = x_ref[...]
    q = _dot(_rms(x, g_ref[...]).astype(BF16), wq_ref[...]).astype(BF16)
    heads = []
    for h in range(n_heads):
        sl = slice(h * dh, (h + 1) * dh)
        s = _dot_nt(q[:, sl], k_ref[:, sl]) * (dh ** -0.5)
        m = jnp.max(s, axis=-1, keepdims=True)
        p = jnp.exp(s - m)
        p = p / jnp.sum(p, axis=-1, keepdims=True)
        heads.append(_dot(p.astype(BF16), v_ref[:, sl]).astype(BF16))
    o_ref[...] = x + _dot(jnp.concatenate(heads, axis=1), wo_ref[...])


def cross_block_prompt(x, g, wq, mkv, wo, *, batch, seq, n_mem, n_heads, dh):
    d = x.shape[1]
    w = n_heads * dh
    tq = _pick(seq, (512, 256, 128))
    nq = seq // tq
    row_blk = pl.BlockSpec((tq, d), lambda b, i: (b * nq + i, 0))
    return pl.pallas_call(
        functools.partial(_cross_block_kernel, n_heads=n_heads, dh=dh),
        grid=(batch, nq),
        in_specs=[row_blk,
                  pl.BlockSpec((1, d), lambda b, i: (0, 0)),
                  pl.BlockSpec((d, w), lambda b, i: (0, 0)),
                  pl.BlockSpec((n_mem, w), lambda b, i: (b, 0)),
                  pl.BlockSpec((n_mem, w), lambda b, i: (b, 1)),
                  pl.BlockSpec((w, d), lambda b, i: (0, 0))],
        out_specs=row_blk,
        out_shape=jax.ShapeDtypeStruct(x.shape, F32),
        compiler_params=_params("parallel", "parallel"),
        name="cross_block_prompt",
    )(x, g.reshape(1, d), wq, mkv, mkv, wo)


def _diff_decode_kernel(*refs, n_pages, page, n_heads, dh, lam_init):
    pos = 1
    qt_ref, knt_ref, vn_ref = refs[pos:pos + 3]; pos += 3
    kt_refs = refs[pos:pos + n_pages]; pos += n_pages
    v_refs = refs[pos:pos + n_pages]; pos += n_pages
    slopes_ref = refs[pos]; pos += 1
    lparams = refs[pos:pos + 4]; pos += 4
    subln_ref = refs[pos]; pos += 1
    o_ref = refs[pos]

    b = pl.program_id(0)
    nb = qt_ref.shape[1]
    onehot = (lax.broadcasted_iota(jnp.int32, (nb, V7X_LANES), 0) == b).astype(BF16)
    q_rep = _dot(qt_ref[...], onehot) * (dh ** -0.5)
    kn_rep = _dot(knt_ref[...], onehot)

    def map_sums(x):
        x3 = x.reshape(n_heads, 2 * dh, x.shape[-1])
        return x3[:, :dh, :].sum(axis=1), x3[:, dh:, :].sum(axis=1)

    past = n_pages * page
    tpos = lax.broadcasted_iota(jnp.int32, (n_heads, page), 1).astype(F32)
    nslope = -slopes_ref[...]
    s1, s2 = [], []
    for p in range(n_pages):
        a, c = map_sums(kt_refs[p][...] * q_rep)
        bias = nslope * ((past - p * page) - tpos)
        s1.append(a + bias)
        s2.append(c + bias)
    n1, n2 = map_sums(kn_rep * q_rep)

    def softmax_parts(s_list, s_new):
        m = s_list[0]
        for s in s_list[1:]:
            m = jnp.maximum(m, s)
        m = jnp.maximum(jnp.max(m, axis=-1, keepdims=True), s_new)
        e = [jnp.exp(s - m) for s in s_list]
        e_new = jnp.exp(s_new - m)
        l = e[0]
        for x in e[1:]:
            l = l + x
        l = jnp.sum(l, axis=-1, keepdims=True) + e_new
        return e, e_new, 1.0 / l

    e1, en1, a1 = softmax_parts(s1, n1)
    e2, en2, a2 = softmax_parts(s2, n2)
    a2 = a2 * _diff_lambda(*lparams, lam_init)

    tok = lax.broadcasted_iota(jnp.int32, (page, n_heads, V7X_LANES), 0)
    lane = lax.broadcasted_iota(jnp.int32, (page, n_heads, V7X_LANES), 2)
    diag = tok == lane
    ones = jnp.ones((V7X_LANES, V7X_LANES), BF16)
    o = (en1 * a1 - en2 * a2) * vn_ref[...]
    for p in range(n_pages):
        w = e1[p] * a1 - e2[p] * a2
        wd = jnp.where(diag, w[None], 0.0).reshape(page * n_heads, V7X_LANES).astype(BF16)
        w3 = _dot(wd, ones).reshape(page, n_heads, V7X_LANES)
        o = o + jnp.sum(w3 * v_refs[p][...], axis=0)
    o_ref[...] = (_rms(o, subln_ref[...]) * (1.0 - lam_init)).astype(o_ref.dtype)


def diff_attn_sample(q, k_new, v_new, cache_kt, cache_v, layer, page_table, slopes, lparams, subln, *,
                     n_heads, dh, dv, lam_init):
    nb, n_pages = page_table.shape
    page = cache_v.shape[2]
    wk = n_heads * 2 * dh
    assert page == V7X_LANES and dv == V7X_LANES and n_heads == V7X_SUBLANES
    const = lambda shape: pl.BlockSpec(shape, lambda b, pt: (0,) * len(shape))

    def kt_spec(p):
        return pl.BlockSpec((None, None, wk, page), lambda b, pt: (layer, pt[b, p], 0, 0))

    def v_spec(p):
        return pl.BlockSpec((None, None, page, n_heads, dv), lambda b, pt: (layer, pt[b, p], 0, 0, 0))

    in_specs = ([const((wk, nb)), const((wk, nb)),
                 pl.BlockSpec((None, n_heads, dv), lambda b, pt: (b, 0, 0))]
                + [kt_spec(p) for p in range(n_pages)] + [v_spec(p) for p in range(n_pages)]
                + [const((n_heads, V7X_LANES))] + [const((1, dh))] * 4 + [const((1, dv))])
    out = pl.pallas_call(
        functools.partial(_diff_decode_kernel, n_pages=n_pages, page=page, n_heads=n_heads, dh=dh,
                          lam_init=lam_init),
        grid_spec=pltpu.PrefetchScalarGridSpec(
            num_scalar_prefetch=1, grid=(nb,), in_specs=in_specs,
            out_specs=pl.BlockSpec((None, n_heads, dv), lambda b, pt: (b, 0, 0))),
        out_shape=jax.ShapeDtypeStruct((nb, n_heads, dv), BF16),
        compiler_params=_params("arbitrary"),
        name="diff_attn_sample",
    )(page_table, q.T.astype(BF16), k_new.T.astype(BF16), v_new.reshape(nb, n_heads, dv),
      *([cache_kt] * n_pages), *([cache_v] * n_pages),
      jnp.broadcast_to(slopes[:, None], (n_heads, V7X_LANES)), *lparams, subln)
    return out.reshape(nb, n_heads * dv)


def _cross_decode_kernel(q_ref, k_ref, v_ref, o_ref, *, n_heads, dh):
    n_req, n_tiles = k_ref.shape[:2]
    ones = jnp.ones((dh, V7X_LANES), BF16)

    def over_tokens(x, op):
        out = x
        for r in range(n_heads, V7X_SUBLANES, n_heads):
            out = op(out, pltpu.roll(x, r, axis=0))
        return out

    for r in range(n_req):
        q = q_ref[r] * (dh ** -0.5)
        prod = (k_ref[r] * q[None]).reshape(n_tiles * V7X_SUBLANES, dh).astype(BF16)
        s = _dot(prod, ones).reshape(n_tiles, V7X_SUBLANES, V7X_LANES)
        m = over_tokens(jnp.max(s, axis=0), jnp.maximum)
        e = jnp.exp(s - m[None])
        l = over_tokens(jnp.sum(e, axis=0), jnp.add)
        o = over_tokens(jnp.sum(e * v_ref[r], axis=0), jnp.add)
        o_ref[r] = o / l


def cross_attn_sample(q, cache_k, cache_v, layer, *, n_heads, dh):
    nb = q.shape[0]
    assert dh == V7X_LANES and V7X_SUBLANES % n_heads == 0
    n_tiles = cache_k.shape[2]
    q8 = jnp.tile(q.reshape(nb, n_heads, dh), (1, V7X_SUBLANES // n_heads, 1))
    rb = _pick(nb, (4, 2, 1))
    kv_spec = pl.BlockSpec((None, rb, n_tiles, V7X_SUBLANES, dh), lambda b: (layer, b, 0, 0, 0))
    row = pl.BlockSpec((rb, V7X_SUBLANES, dh), lambda b: (b, 0, 0))
    out = pl.pallas_call(
        functools.partial(_cross_decode_kernel, n_heads=n_heads, dh=dh),
        grid=(nb // rb,),
        in_specs=[row, kv_spec, kv_spec],
        out_specs=row,
        out_shape=jax.ShapeDtypeStruct((nb, V7X_SUBLANES, dh), F32),
        compiler_params=_params("parallel"),
        name="cross_attn_sample",
    )(q8, cache_k, cache_v)
    return out[:, :n_heads].reshape(nb, n_heads * dh)


def _retention_step_kernel(q_ref, k_ref, v_ref, s_ref, *rest, n_heads, dk, dv, log_gammas):
    o_ref, sn_ref = rest[-2:]
    scale = dk ** -0.5
    eye = (lax.broadcasted_iota(jnp.int32, (dk, dk), 0)
           == lax.broadcasted_iota(jnp.int32, (dk, dk), 1))
    for r in range(q_ref.shape[0]):
        for h in range(n_heads):
            gamma = math.exp(log_gammas[h])
            q = q_ref[r, :, h * dk:(h + 1) * dk]
            k = k_ref[r, :, h * dk:(h + 1) * dk]
            v = v_ref[r, :, h * dv:(h + 1) * dv]
            state = s_ref[r, h]
            inner = jnp.sum(q * k, axis=-1, keepdims=True) * scale
            q8 = jnp.broadcast_to(q, (V7X_SUBLANES, dk)).astype(BF16)
            o_cross = _dot(q8, state.astype(BF16))[0:1] * gamma
            o_ref[r, :, h * dv:(h + 1) * dv] = inner * v + o_cross
            k_diag = jnp.where(eye, jnp.broadcast_to(k * scale, (dk, dk)), 0.0).astype(BF16)
            v_rows = jnp.broadcast_to(v, (dk, dv)).astype(BF16)
            sn_ref[r, h] = gamma * state + _dot(k_diag, v_rows)


def retention_sample(q, k, v, state, layer, stacked, *, n_heads, dk, dv, log_gammas):
    nb = q.shape[0]
    wq, wv = n_heads * dk, n_heads * dv
    rb = _pick(nb, (4, 2, 1))
    row = lambda w: pl.BlockSpec((rb, 1, w), lambda b, *_: (b, 0, 0))
    state_blk = pl.BlockSpec((None, rb, n_heads, dk, dv), lambda b, *_: (layer, b, 0, 0, 0))
    ins = [q.reshape(nb, 1, wq), k.reshape(nb, 1, wq), v.reshape(nb, 1, wv), state]
    in_specs = [row(wq), row(wq), row(wv), state_blk]
    aliases = _stacked(in_specs, ins, stacked, n_out_before=1)
    o, s_new = pl.pallas_call(
        functools.partial(_retention_step_kernel, n_heads=n_heads, dk=dk, dv=dv,
                          log_gammas=log_gammas),
        grid=(nb // rb,),
        in_specs=in_specs,
        out_specs=(row(wv), state_blk),
        out_shape=(jax.ShapeDtypeStruct((nb, 1, wv), F32),
                   jax.ShapeDtypeStruct(state.shape, F32)),
        input_output_aliases=aliases,
        compiler_params=_params("parallel"),
        name="retention_sample",
    )(*ins)
    return o.reshape(nb, wv), s_new


def kernel(x_prompt, x_sample, mem_prompt, cache_k_diff, cache_v_diff, cache_mem_k, cache_mem_v,
           state_ret, page_table, norm_mix, w_in, w_out, da_lq1, da_lk1, da_lq2, da_lk2, da_subln,
           ret_norm, norm_x, norm_mem, wx_q, wx_kv, wx_o, norm_ff, w_up, w_down, norm_final):
    batch, seq, d_model = x_prompt.shape
    dec_batch, dec_seq, _ = x_sample.shape
    assert dec_seq == 1
    depth, n_pool, page, h_da, _, dh_da = cache_k_diff.shape
    dv_da = cache_v_diff.shape[-1]
    _, _, h_ret, dk_ret, dv_ret = state_ret.shape
    _, _, n_mem, h_x, dh_x = cache_mem_k.shape
    d_qk = h_da * 2 * dh_da
    d_da = h_da * dv_da
    d_rqk = h_ret * dk_ret
    d_ret = h_ret * dv_ret
    d_x = h_x * dh_x
    chunk = 128
    mp = batch * seq

    slopes = 2.0 ** (-8.0 * (jnp.arange(h_da, dtype=F32) + 1.0) / h_da)
    log_gammas = tuple(math.log(1.0 - 2.0 ** (-5.0 - h)) for h in range(h_ret))

    o_q, o_k, o_v = 0, d_qk, 2 * d_qk
    o_qr = o_v + d_da
    o_kr, o_vr = o_qr + d_rqk, o_qr + 2 * d_rqk
    o_g = o_vr + d_ret

    cache_kt = jnp.transpose(cache_k_diff, (0, 1, 3, 4, 5, 2)).reshape(depth, n_pool, d_qk, page)
    mem_tiles = n_mem * h_x // V7X_SUBLANES
    mem_k = cache_mem_k.reshape(depth, dec_batch, mem_tiles, V7X_SUBLANES, dh_x)
    mem_v = cache_mem_v.reshape(depth, dec_batch, mem_tiles, V7X_SUBLANES, dh_x)

    xp = x_prompt.reshape(mp, d_model)
    xs = x_sample.reshape(dec_batch, d_model)
    mem = mem_prompt.reshape(batch * n_mem, d_model)

    outs = {name: [] for name in ("pk", "ps", "pmk", "pmv", "sk", "sv")}
    pv_all = ss_all = None
    for l in range(depth):
        lam_init = 0.8 - 0.6 * math.exp(-0.3 * l)
        lparams = tuple(a[l].reshape(1, dh_da) for a in (da_lq1, da_lk1, da_lq2, da_lk2))
        subln = da_subln[l].reshape(1, dv_da)
        w_in_l = w_in[l]
        w_a = jnp.concatenate([w_in_l[:, o_q:o_k], w_in_l[:, o_qr:o_g]], axis=1).astype(BF16)
        w_k = w_in_l[:, o_k:o_v].astype(BF16)
        w_v = w_in_l[:, o_v:o_qr].astype(BF16)
        w_g = w_in_l[:, o_g:].astype(BF16)
        w_out_da = w_out[l, :d_da].astype(BF16)
        w_out_r = w_out[l, d_da:].astype(BF16)
        wq_b, wkv_b, wo_b = wx_q[l].astype(BF16), wx_kv[l].astype(BF16), wx_o[l].astype(BF16)
        wu_b, wd_b = w_up[l].astype(BF16), w_down[l].astype(BF16)
        a_q, a_qr, a_kr, a_vr = 0, d_qk, d_qk + d_rqk, d_qk + 2 * d_rqk

        last = l == depth - 1
        g_next = norm_final if last else norm_mix[l + 1]

        h = rmsnorm(xp, norm_mix[l], BF16) if l == 0 else hp_next
        (pa,) = matmul([(h, w_a)], (BF16,))
        k_f32, k_b = matmul([(h, w_k)], (F32, BF16))
        pv_all, v_b = vproj(h, w_v, l, depth, pv_all, n_heads=h_da)
        (g_r,) = matmul([(h, w_g)], (F32,))
        y_da = diff_attn_prompt(pa, k_b, v_b, slopes, lparams, subln,
                                batch=batch, seq=seq, n_heads=h_da, dh=dh_da, dv=dv_da, lam_init=lam_init)
        y_r, s_p = retention_prompt(pa, g_r, ret_norm[l], batch=batch, seq=seq, n_heads=h_ret,
                                    dk=dk_ret, dv=dv_ret, chunk=chunk, log_gammas=log_gammas,
                                    q_col=a_qr, k_col=a_kr, v_col=a_vr)
        (xp,) = matmul([(y_da, w_out_da), (y_r, w_out_r)], (F32,), residual=xp)
        hm = rmsnorm(mem, norm_mem[l], BF16)
        mkv_f32, mkv_b = matmul([(hm, wkv_b)], (F32, BF16))
        xp = cross_block_prompt(xp, norm_x[l], wq_b, mkv_b, wo_b, batch=batch, seq=seq, n_mem=n_mem,
                                n_heads=h_x, dh=dh_x)
        if last:
            (y_prompt,) = mlp(xp, norm_ff[l], wu_b, wd_b, g_next, keep_x=False)
        else:
            xp, hp_next = mlp(xp, norm_ff[l], wu_b, wd_b, g_next, keep_x=True)
        outs["pk"].append(k_f32.reshape(batch, seq, h_da, 2, dh_da))
        outs["ps"].append(s_p)
        outs["pmk"].append(mkv_f32[:, :d_x].reshape(batch, n_mem, h_x, dh_x))
        outs["pmv"].append(mkv_f32[:, d_x:].reshape(batch, n_mem, h_x, dh_x))

        h = rmsnorm(xs, norm_mix[l], BF16) if l == 0 else hs_next
        (ps_a,) = matmul([(h, w_a)], (F32,))
        (k_new,) = matmul([(h, w_k)], (F32,))
        (v_new,) = matmul([(h, w_v)], (F32,))
        (g_s,) = matmul([(h, w_g)], (F32,))
        y_da = diff_attn_sample(ps_a[:, a_q:a_qr], k_new, v_new, cache_kt, cache_v_diff, l, page_table,
                                slopes, lparams, subln, n_heads=h_da, dh=dh_da, dv=dv_da,
                                lam_init=lam_init)
        o_r, ss_all = retention_sample(ps_a[:, a_qr:a_kr], ps_a[:, a_kr:a_vr], ps_a[:, a_vr:], state_ret, l, ss_all,
                                    n_heads=h_ret, dk=dk_ret, dv=dv_ret, log_gammas=log_gammas)
        y_r = gated_headnorm(o_r.reshape(dec_batch * h_ret, dv_ret), ret_norm[l],
                             g_s.reshape(dec_batch * h_ret, dv_ret))
        (xs,) = matmul([(y_da, w_out_da), (y_r.reshape(dec_batch, d_ret), w_out_r)],
                       (F32,), residual=xs)
        hx = rmsnorm(xs, norm_x[l], BF16)
        (qx,) = matmul([(hx, wq_b)], (F32,))
        ox = cross_attn_sample(qx, mem_k, mem_v, l, n_heads=h_x, dh=dh_x)
        (xs,) = matmul([(ox.astype(BF16), wo_b)], (F32,), residual=xs)
        if last:
            (y_sample,) = mlp(xs, norm_ff[l], wu_b, wd_b, g_next, keep_x=False)
        else:
            xs, hs_next = mlp(xs, norm_ff[l], wu_b, wd_b, g_next, keep_x=True)
        outs["sk"].append(k_new.reshape(dec_batch, 1, h_da, 2, dh_da))
        outs["sv"].append(v_new.reshape(dec_batch, 1, h_da, dv_da))

    y_prompt = y_prompt.reshape(batch, seq, d_model)
    y_sample = y_sample.reshape(dec_batch, 1, d_model)
    st = lambda name: jnp.stack(outs[name])
    return (y_prompt, y_sample, st("pk"), pv_all.reshape(depth, batch, seq, h_da, dv_da), st("ps"),
            st("pmk"), st("pmv"), st("sk"), st("sv"), ss_all)
```

```python
import functools
import math

import jax
import jax.numpy as jnp
from jax import lax
from jax.experimental import pallas as pl
from jax.experimental.pallas import tpu as pltpu

EPS = 1e-5
NEG_INF = -1e30
LOG2E = math.log2(math.e)
F32 = jnp.float32
BF16 = jnp.bfloat16

V7X_VMEM_LIMIT_BYTES = 60 * 1024 * 1024
V7X_LANES = 128
V7X_SUBLANES = 8


def _params(*semantics):
    return pltpu.CompilerParams(dimension_semantics=semantics,
                                vmem_limit_bytes=V7X_VMEM_LIMIT_BYTES)


def _pick(n, candidates):
    for c in candidates:
        if n % c == 0:
            return c
    return n


def _dot(a, b):
    return jnp.dot(a, b, preferred_element_type=F32)


def _dot_nt(a, b):
    return lax.dot_general(a, b, (((1,), (1,)), ((), ())), preferred_element_type=F32)


def _rms(x, g):
    ms = jnp.mean(x * x, axis=-1, keepdims=True)
    return x * lax.rsqrt(ms + EPS) * g


def _rmsnorm_kernel(x_ref, g_ref, o_ref):
    o_ref[...] = _rms(x_ref[...], g_ref[...]).astype(o_ref.dtype)


def rmsnorm(x, g, out_dtype):
    m, d = x.shape
    tm = _pick(m, (512, 256, 128))
    return pl.pallas_call(
        _rmsnorm_kernel,
        grid=(m // tm,),
        in_specs=[pl.BlockSpec((tm, d), lambda i: (i, 0)),
                  pl.BlockSpec((1, d), lambda i: (0, 0))],
        out_specs=pl.BlockSpec((tm, d), lambda i: (i, 0)),
        out_shape=jax.ShapeDtypeStruct((m, d), out_dtype),
        compiler_params=_params("parallel"),
        name="rmsnorm",
    )(x, g.reshape(1, d))


def _gated_headnorm_kernel(x_ref, g_ref, gate_ref, o_ref):
    gate = gate_ref[...]
    y = _rms(x_ref[...], g_ref[...]) * (gate * jax.nn.sigmoid(gate))
    o_ref[...] = y.astype(o_ref.dtype)


def gated_headnorm(x, g, gate):
    r, d = x.shape
    tr = _pick(r, (512, 256, 128))
    blk = pl.BlockSpec((tr, d), lambda i: (i, 0))
    return pl.pallas_call(
        _gated_headnorm_kernel,
        grid=(r // tr,),
        in_specs=[blk, pl.BlockSpec((1, d), lambda i: (0, 0)), blk],
        out_specs=blk,
        out_shape=jax.ShapeDtypeStruct((r, d), BF16),
        compiler_params=_params("parallel"),
        name="gated_headnorm",
    )(x, g.reshape(1, d), gate)


def _mm_kernel(*refs, n_pairs, has_res):
    acc = _dot(refs[0][...], refs[1][...])
    for p in range(1, n_pairs):
        acc = acc + _dot(refs[2 * p][...], refs[2 * p + 1][...])
    pos = 2 * n_pairs
    if has_res:
        acc = refs[pos][...] + acc
        pos += 1
    for o_ref in refs[pos:]:
        o_ref[...] = acc.astype(o_ref.dtype)


def matmul(pairs, out_dtypes, residual=None):
    m = pairs[0][0].shape[0]
    n = pairs[0][1].shape[1]
    tm = _pick(m, (1024, 512, 256, 128))
    tn = _pick(n, (1024, 512, 256, 128))
    ins, specs = [], []
    for a, w in pairs:
        k = a.shape[1]
        ins += [a, w]
        specs += [pl.BlockSpec((tm, k), lambda j, i: (i, 0)),
                  pl.BlockSpec((k, tn), lambda j, i: (0, j))]
    if residual is not None:
        ins.append(residual)
        specs.append(pl.BlockSpec((tm, tn), lambda j, i: (i, j)))
    return pl.pallas_call(
        functools.partial(_mm_kernel, n_pairs=len(pairs), has_res=residual is not None),
        grid=(n // tn, m // tm),
        in_specs=specs,
        out_specs=tuple(pl.BlockSpec((tm, tn), lambda j, i: (i, j)) for _ in out_dtypes),
        out_shape=tuple(jax.ShapeDtypeStruct((m, n), dt) for dt in out_dtypes),
        compiler_params=_params("parallel", "parallel"),
        name="matmul",
    )(*ins)


def _vproj_kernel(a_ref, w_ref, *rest, n_heads):
    o4_ref, ob_ref = rest[-2:]
    acc = _dot(a_ref[...], w_ref[...])
    ob_ref[...] = acc.astype(BF16)
    dv = acc.shape[1] // n_heads
    for h in range(n_heads):
        o4_ref[:, h, :] = acc[:, h * dv:(h + 1) * dv]


def vproj(a, w, layer, depth, stacked, *, n_heads):
    m, k = a.shape
    n = w.shape[1]
    dv = n // n_heads
    tm = _pick(m, (512, 256, 128))
    ins = [a, w]
    in_specs = [pl.BlockSpec((tm, k), lambda i: (i, 0)), pl.BlockSpec((k, n), lambda i: (0, 0))]
    aliases = _stacked(in_specs, ins, stacked)
    return pl.pallas_call(
        functools.partial(_vproj_kernel, n_heads=n_heads),
        grid=(m // tm,),
        in_specs=in_specs,
        out_specs=(pl.BlockSpec((None, tm, n_heads, dv), lambda i: (layer, i, 0, 0)),
                   pl.BlockSpec((tm, n), lambda i: (i, 0))),
        out_shape=(jax.ShapeDtypeStruct((depth, m, n_heads, dv), F32),
                   jax.ShapeDtypeStruct((m, n), BF16)),
        input_output_aliases=aliases,
        compiler_params=_params("parallel"),
        name="vproj",
    )(*ins)


def _stacked(spec_list, ins, stacked, n_out_before=0):
    if stacked is None:
        return {}
    ins.append(stacked)
    spec_list.append(pl.BlockSpec(memory_space=pl.ANY))
    return {len(ins) - 1: n_out_before}


def _mlp_kernel(x_ref, g_ref, wu_ref, wd_ref, gn_ref, *refs, keep_x):
    acc_ref, h_ref = refs[0], refs[-1]
    f = pl.program_id(1)

    @pl.when(f == 0)
    def _():
        h_ref[...] = _rms(x_ref[...], g_ref[...]).astype(BF16)

    u = _dot(h_ref[...], wu_ref[...])
    a = jnp.square(jnp.maximum(u, 0.0)).astype(BF16)
    c = _dot(a, wd_ref[...])

    @pl.when(f == 0)
    def _():
        acc_ref[...] = x_ref[...] + c

    @pl.when(f != 0)
    def _():
        acc_ref[...] += c

    @pl.when(f == pl.num_programs(1) - 1)
    def _():
        n_ref = refs[1] if keep_x else acc_ref
        n_ref[...] = _rms(acc_ref[...], gn_ref[...]).astype(n_ref.dtype)


def mlp(x, g, w_up, w_down, layer, g_next, *, keep_x):
    m, d = x.shape
    ff = w_up.shape[2]
    tm = _pick(m, (512, 256, 128))
    tf = _pick(ff, (2048, 1024, 512, 256, 128))
    row_blk = pl.BlockSpec((tm, d), lambda i, f: (i, 0))
    vec = pl.BlockSpec((1, d), lambda i, f: (0, 0))
    out_shape = [jax.ShapeDtypeStruct((m, d), F32)]
    if keep_x:
        out_shape.append(jax.ShapeDtypeStruct((m, d), BF16))
    return pl.pallas_call(
        functools.partial(_mlp_kernel, keep_x=keep_x),
        grid=(m // tm, ff // tf),
        in_specs=[row_blk, vec,
                  pl.BlockSpec((None, d, tf), lambda i, f: (layer, 0, f)),
                  pl.BlockSpec((None, tf, d), lambda i, f: (layer, f, 0)),
                  vec],
        out_specs=tuple(row_blk for _ in out_shape),
        out_shape=tuple(out_shape),
        scratch_shapes=[pltpu.VMEM((tm, d), BF16)],
        compiler_params=_params("parallel", "arbitrary"),
        name="mlp",
    )(x, g.reshape(1, d), w_up, w_down, g_next.reshape(1, d))


def _diff_lambda(lq1_ref, lk1_ref, lq2_ref, lk2_ref, lam_init):
    e1 = jnp.exp(jnp.sum(lq1_ref[...] * lk1_ref[...], axis=-1, keepdims=True))
    e2 = jnp.exp(jnp.sum(lq2_ref[...] * lk2_ref[...], axis=-1, keepdims=True))
    return e1 - e2 + lam_init


def _lane_fold(x, op):
    out = x[:, :V7X_LANES]
    for c in range(V7X_LANES, x.shape[1], V7X_LANES):
        out = op(out, x[:, c:c + V7X_LANES])
    return out


def _diff_attn_kernel(slopes_ref, q_ref, k_ref, v_ref, lq1_ref, lk1_ref, lq2_ref, lk2_ref,
                      subln_ref, o_ref, s_ref, m_ref, l_ref, acc_ref, *, t, dh, dv, group, lam_init):
    hg = pl.program_id(1)
    i = pl.program_id(2)
    kpos0 = lax.broadcasted_iota(jnp.int32, (1, t), 1).astype(F32)
    lane = lax.broadcasted_iota(jnp.int32, (t, 2 * dh), 1)

    qq, slope2 = [], []
    for g in range(group):
        q = q_ref[:, g * 2 * dh:(g + 1) * 2 * dh].astype(F32) * (dh ** -0.5 * LOG2E)
        qq.append(jnp.concatenate([jnp.where(lane < dh, q, 0.0), jnp.where(lane >= dh, q, 0.0)],
                                  axis=0).astype(BF16))
        slope2.append(slopes_ref[hg * group + g] * LOG2E)

    def scores(g, j):
        start = pl.multiple_of(j * t, t)
        kpos = kpos0 + (j * t).astype(F32)
        return _dot_nt(qq[g], k_ref[pl.ds(start, t), g * 2 * dh:(g + 1) * 2 * dh]) + slope2[g] * kpos

    def pass1(j, carry):
        for g in range(group):
            s = scores(g, j)
            s_ref[g, :, pl.ds(pl.multiple_of(j * t, t), t)] = s
            m_ref[g] = jnp.maximum(m_ref[g], _lane_fold(s, jnp.maximum))
        return carry

    row = lax.broadcasted_iota(jnp.int32, (2 * t, t), 0)
    col = lax.broadcasted_iota(jnp.int32, (2 * t, t), 1)
    visible = jnp.where(row >= t, row - t, row) >= col
    for g in range(group):
        s = jnp.where(visible, scores(g, i), NEG_INF)
        s_ref[g, :, pl.ds(pl.multiple_of(i * t, t), t)] = s
        m_ref[g] = _lane_fold(s, jnp.maximum)
    lax.fori_loop(0, i, pass1, 0)

    for g in range(group):
        m = jnp.max(m_ref[g], axis=-1, keepdims=True)
        m_ref[g] = jnp.broadcast_to(m, (2 * t, V7X_LANES))
    l_ref[...] = jnp.zeros_like(l_ref)
    acc_ref[...] = jnp.zeros_like(acc_ref)

    def pass2(j, carry):
        start = pl.multiple_of(j * t, t)
        for g in range(group):
            mb = m_ref[g]
            ps = []
            lsum = l_ref[g]
            for c in range(0, t, V7X_LANES):
                p = jnp.exp2(s_ref[g, :, pl.ds(start + c, V7X_LANES)] - mb)
                lsum = lsum + p
                ps.append(p.astype(BF16))
            l_ref[g] = lsum
            acc_ref[g] += _dot(jnp.concatenate(ps, axis=1), v_ref[pl.ds(start, t), g * dv:(g + 1) * dv])
        return carry

    lax.fori_loop(0, i + 1, pass2, 0)

    lam = _diff_lambda(lq1_ref, lk1_ref, lq2_ref, lk2_ref, lam_init)
    for g in range(group):
        o = acc_ref[g] / jnp.sum(l_ref[g], axis=-1, keepdims=True)
        o = o[:t] - lam * o[t:]
        o_ref[:, g * dv:(g + 1) * dv] = (_rms(o, subln_ref[...]) * (1.0 - lam_init)).astype(o_ref.dtype)


def diff_attn_prompt(q, k, v, slopes, lparams, subln, *, batch, seq, n_heads, dh, dv, lam_init):
    t = _pick(seq, (512, 256, 128))
    nq = seq // t
    group = _pick(n_heads, (4, 2, 1))
    vec = lambda n: pl.BlockSpec((1, n), lambda b, h, i: (0, 0))
    return pl.pallas_call(
        functools.partial(_diff_attn_kernel, t=t, dh=dh, dv=dv, group=group, lam_init=lam_init),
        grid=(batch, n_heads // group, nq),
        in_specs=[pl.BlockSpec(memory_space=pltpu.SMEM),
                  pl.BlockSpec((t, group * 2 * dh), lambda b, h, i: (b * nq + i, h)),
                  pl.BlockSpec((seq, group * 2 * dh), lambda b, h, i: (b, h)),
                  pl.BlockSpec((seq, group * dv), lambda b, h, i: (b, h)),
                  vec(dh), vec(dh), vec(dh), vec(dh), vec(dv)],
        out_specs=pl.BlockSpec((t, group * dv), lambda b, h, i: (b * nq + i, h)),
        out_shape=jax.ShapeDtypeStruct((batch * seq, n_heads * dv), BF16),
        scratch_shapes=[pltpu.VMEM((group, 2 * t, seq), F32),
                        pltpu.VMEM((group, 2 * t, V7X_LANES), F32),
                        pltpu.VMEM((group, 2 * t, V7X_LANES), F32),
                        pltpu.VMEM((group, 2 * t, dv), F32)],
        compiler_params=_params("parallel", "parallel", "arbitrary"),
        name="diff_attn_prompt",
    )(slopes, q, k, v, *lparams, subln)


def _retention_kernel(q_ref, k_ref, v_ref, g_ref, norm_ref, y_ref, s_ref, *, c, n_heads, dk, dv,
                      log_gammas):
    step = pl.program_id(1)

    @pl.when(step == 0)
    def _():
        s_ref[...] = jnp.zeros_like(s_ref)

    row = lax.broadcasted_iota(jnp.int32, (c, c), 0)
    col = lax.broadcasted_iota(jnp.int32, (c, c), 1)
    diff = (row - col).astype(F32)
    n_v = lax.broadcasted_iota(jnp.int32, (c, dv), 0).astype(F32)
    n_k = lax.broadcasted_iota(jnp.int32, (c, dk), 0).astype(F32)
    scale = dk ** -0.5
    for h in range(n_heads):
        lg = log_gammas[h]
        q = q_ref[:, h * dk:(h + 1) * dk]
        k = k_ref[:, h * dk:(h + 1) * dk]
        v = v_ref[:, h * dv:(h + 1) * dv]
        state = s_ref[h]
        dmask = jnp.where(diff >= 0, jnp.exp(lg * jnp.maximum(diff, 0.0)), 0.0) * scale
        inner = _dot_nt(q, k) * dmask
        o = _dot(inner.astype(BF16), v)
        o = o + _dot(q, state.astype(BF16)) * jnp.exp(lg * (n_v + 1.0))
        k_dec = (k.astype(F32) * (jnp.exp(lg * (c - 1.0 - n_k)) * scale)).T
        s_ref[h] = math.exp(lg * c) * state + _dot(k_dec.astype(BF16), v)
        gate = g_ref[:, h * dv:(h + 1) * dv]
        y = _rms(o, norm_ref[...]) * (gate * jax.nn.sigmoid(gate))
        y_ref[:, h * dv:(h + 1) * dv] = y.astype(y_ref.dtype)


def retention_prompt(pa, g_r, ret_norm, *, batch, seq, n_heads, dk, dv, chunk, log_gammas,
                     q_col, k_col, v_col):
    nc = seq // chunk
    wq = n_heads * dk
    wv = n_heads * dv
    return pl.pallas_call(
        functools.partial(_retention_kernel, c=chunk, n_heads=n_heads, dk=dk, dv=dv,
                          log_gammas=log_gammas),
        grid=(batch, nc),
        in_specs=[pl.BlockSpec((chunk, wq), lambda b, s: (b * nc + s, q_col // wq)),
                  pl.BlockSpec((chunk, wq), lambda b, s: (b * nc + s, k_col // wq)),
                  pl.BlockSpec((chunk, wv), lambda b, s: (b * nc + s, v_col // wv)),
                  pl.BlockSpec((chunk, wv), lambda b, s: (b * nc + s, 0)),
                  pl.BlockSpec((1, dv), lambda b, s: (0, 0))],
        out_specs=(pl.BlockSpec((chunk, wv), lambda b, s: (b * nc + s, 0)),
                   pl.BlockSpec((None, n_heads, dk, dv), lambda b, s: (b, 0, 0, 0))),
        out_shape=(jax.ShapeDtypeStruct((batch * seq, wv), BF16),
                   jax.ShapeDtypeStruct((batch, n_heads, dk, dv), F32)),
        compiler_params=_params("parallel", "arbitrary"),
        name="retention_prompt",
    )(pa, pa, pa, g_r, ret_norm.reshape(1, dv))


def _cross_block_kernel(x_ref, g_ref, wq_ref, k_ref, v_ref, wo_ref, o_ref, *, n_heads, dh):
    x = x_ref[...]
    q = _dot(_rms(x, g_ref[...]).astype(BF16), wq_ref[...]).astype(BF16)
    heads = []
    for h in range(n_heads):
        sl = slice(h * dh, (h + 1) * dh)
        s = _dot_nt(q[:, sl], k_ref[:, sl]) * (dh ** -0.5)
        m = jnp.max(s, axis=-1, keepdims=True)
        p = jnp.exp(s - m)
        p = p / jnp.sum(p, axis=-1, keepdims=True)
        heads.append(_dot(p.astype(BF16), v_ref[:, sl]).astype(BF16))
    o_ref[...] = x + _dot(jnp.concatenate(heads, axis=1), wo_ref[...])


def cross_block_prompt(x, g, wq, mkv, wo, *, batch, seq, n_mem, n_heads, dh):
    d = x.shape[1]
    w = n_heads * dh
    tq = _pick(seq, (512, 256, 128))
    nq = seq // tq
    row_blk = pl.BlockSpec((tq, d), lambda b, i: (b * nq + i, 0))
    return pl.pallas_call(
        functools.partial(_cross_block_kernel, n_heads=n_heads, dh=dh),
        grid=(batch, nq),
        in_specs=[row_blk,
                  pl.BlockSpec((1, d), lambda b, i: (0, 0)),
                  pl.BlockSpec((d, w), lambda b, i: (0, 0)),
                  pl.BlockSpec((n_mem, w), lambda b, i: (b, 0)),
                  pl.BlockSpec((n_mem, w), lambda b, i: (b, 1)),
                  pl.BlockSpec((w, d), lambda b, i: (0, 0))],
        out_specs=row_blk,
        out_shape=jax.ShapeDtypeStruct(x.shape, F32),
        compiler_params=_params("parallel", "parallel"),
        name="cross_block_prompt",
    )(x, g.reshape(1, d), wq, mkv, mkv, wo)


def _diff_decode_kernel(*refs, n_pages, page, n_heads, dh, lam_init):
    pos = 1
    qt_ref, knt_ref, vn_ref = refs[pos:pos + 3]; pos += 3
    kt_refs = refs[pos:pos + n_pages]; pos += n_pages
    v_refs = refs[pos:pos + n_pages]; pos += n_pages
    slopes_ref = refs[pos]; pos += 1
    lparams = refs[pos:pos + 4]; pos += 4
    subln_ref = refs[pos]; pos += 1
    o_ref = refs[pos]

    b = pl.program_id(0)
    nb = qt_ref.shape[1]
    onehot = (lax.broadcasted_iota(jnp.int32, (nb, V7X_LANES), 0) == b).astype(BF16)
    q_rep = _dot(qt_ref[...], onehot) * (dh ** -0.5)
    kn_rep = _dot(knt_ref[...], onehot)

    def map_sums(x):
        x3 = x.reshape(n_heads, 2 * dh, x.shape[-1])
        return x3[:, :dh, :].sum(axis=1), x3[:, dh:, :].sum(axis=1)

    past = n_pages * page
    tpos = lax.broadcasted_iota(jnp.int32, (n_heads, page), 1).astype(F32)
    nslope = -slopes_ref[...]
    s1, s2 = [], []
    for p in range(n_pages):
        a, c = map_sums(kt_refs[p][...] * q_rep)
        bias = nslope * ((past - p * page) - tpos)
        s1.append(a + bias)
        s2.append(c + bias)
    n1, n2 = map_sums(kn_rep * q_rep)

    def softmax_parts(s_list, s_new):
        m = s_list[0]
        for s in s_list[1:]:
            m = jnp.maximum(m, s)
        m = jnp.maximum(jnp.max(m, axis=-1, keepdims=True), s_new)
        e = [jnp.exp(s - m) for s in s_list]
        e_new = jnp.exp(s_new - m)
        l = e[0]
        for x in e[1:]:
            l = l + x
        l = jnp.sum(l, axis=-1, keepdims=True) + e_new
        return e, e_new, 1.0 / l

    e1, en1, a1 = softmax_parts(s1, n1)
    e2, en2, a2 = softmax_parts(s2, n2)
    a2 = a2 * _diff_lambda(*lparams, lam_init)

    tok = lax.broadcasted_iota(jnp.int32, (page, n_heads, V7X_LANES), 0)
    lane = lax.broadcasted_iota(jnp.int32, (page, n_heads, V7X_LANES), 2)
    diag = tok == lane
    ones = jnp.ones((V7X_LANES, V7X_LANES), BF16)
    o = (en1 * a1 - en2 * a2) * vn_ref[...]
    for p in range(n_pages):
        w = e1[p] * a1 - e2[p] * a2
        wd = jnp.where(diag, w[None], 0.0).reshape(page * n_heads, V7X_LANES).astype(BF16)
        w3 = _dot(wd, ones).reshape(page, n_heads, V7X_LANES)
        o = o + jnp.sum(w3 * v_refs[p][...], axis=0)
    o_ref[...] = (_rms(o, subln_ref[...]) * (1.0 - lam_init)).astype(o_ref.dtype)


def diff_attn_sample(q, k_new, v_new, cache_kt, cache_v, layer, page_table, slopes, lparams, subln, *,
                     n_heads, dh, dv, lam_init):
    nb, n_pages = page_table.shape
    page = cache_v.shape[2]
    wk = n_heads * 2 * dh
    assert page == V7X_LANES and dv == V7X_LANES and n_heads == V7X_SUBLANES
    const = lambda shape: pl.BlockSpec(shape, lambda b, pt: (0,) * len(shape))

    def kt_spec(p):
        return pl.BlockSpec((None, None, wk, page), lambda b, pt: (layer, pt[b, p], 0, 0))

    def v_spec(p):
        return pl.BlockSpec((None, None, page, n_heads, dv), lambda b, pt: (layer, pt[b, p], 0, 0, 0))

    in_specs = ([const((wk, nb)), const((wk, nb)),
                 pl.BlockSpec((None, n_heads, dv), lambda b, pt: (b, 0, 0))]
                + [kt_spec(p) for p in range(n_pages)] + [v_spec(p) for p in range(n_pages)]
                + [const((n_heads, V7X_LANES))] + [const((1, dh))] * 4 + [const((1, dv))])
    out = pl.pallas_call(
        functools.partial(_diff_decode_kernel, n_pages=n_pages, page=page, n_heads=n_heads, dh=dh,
                          lam_init=lam_init),
        grid_spec=pltpu.PrefetchScalarGridSpec(
            num_scalar_prefetch=1, grid=(nb,), in_specs=in_specs,
            out_specs=pl.BlockSpec((None, n_heads, dv), lambda b, pt: (b, 0, 0))),
        out_shape=jax.ShapeDtypeStruct((nb, n_heads, dv), BF16),
        compiler_params=_params("arbitrary"),
        name="diff_attn_sample",
    )(page_table, q.T.astype(BF16), k_new.T.astype(BF16), v_new.reshape(nb, n_heads, dv),
      *([cache_kt] * n_pages), *([cache_v] * n_pages),
      jnp.broadcast_to(slopes[:, None], (n_heads, V7X_LANES)), *lparams, subln)
    return out.reshape(nb, n_heads * dv)


def _cross_decode_kernel(q_ref, k_ref, v_ref, o_ref, *, n_heads, dh):
    n_req, n_tiles = k_ref.shape[:2]
    ones = jnp.ones((dh, V7X_LANES), BF16)

    def over_tokens(x, op):
        out = x
        for r in range(n_heads, V7X_SUBLANES, n_heads):
            out = op(out, pltpu.roll(x, r, axis=0))
        return out

    for r in range(n_req):
        q = q_ref[r] * (dh ** -0.5)
        prod = (k_ref[r] * q[None]).reshape(n_tiles * V7X_SUBLANES, dh).astype(BF16)
        s = _dot(prod, ones).reshape(n_tiles, V7X_SUBLANES, V7X_LANES)
        m = over_tokens(jnp.max(s, axis=0), jnp.maximum)
        e = jnp.exp(s - m[None])
        l = over_tokens(jnp.sum(e, axis=0), jnp.add)
        o = over_tokens(jnp.sum(e * v_ref[r], axis=0), jnp.add)
        o_ref[r] = o / l


def cross_attn_sample(q, cache_k, cache_v, layer, *, n_heads, dh):
    nb = q.shape[0]
    assert dh == V7X_LANES and V7X_SUBLANES % n_heads == 0
    n_tiles = cache_k.shape[2]
    q8 = jnp.tile(q.reshape(nb, n_heads, dh), (1, V7X_SUBLANES // n_heads, 1))
    rb = _pick(nb, (4, 2, 1))
    kv_spec = pl.BlockSpec((None, rb, n_tiles, V7X_SUBLANES, dh), lambda b: (layer, b, 0, 0, 0))
    row = pl.BlockSpec((rb, V7X_SUBLANES, dh), lambda b: (b, 0, 0))
    out = pl.pallas_call(
        functools.partial(_cross_decode_kernel, n_heads=n_heads, dh=dh),
        grid=(nb // rb,),
        in_specs=[row, kv_spec, kv_spec],
        out_specs=row,
        out_shape=jax.ShapeDtypeStruct((nb, V7X_SUBLANES, dh), F32),
        compiler_params=_params("parallel"),
        name="cross_attn_sample",
    )(q8, cache_k, cache_v)
    return out[:, :n_heads].reshape(nb, n_heads * dh)


def _retention_step_kernel(q_ref, k_ref, v_ref, s_ref, *rest, n_heads, dk, dv, log_gammas):
    o_ref, sn_ref = rest[-2:]
    scale = dk ** -0.5
    eye = (lax.broadcasted_iota(jnp.int32, (dk, dk), 0)
           == lax.broadcasted_iota(jnp.int32, (dk, dk), 1))
    for r in range(q_ref.shape[0]):
        for h in range(n_heads):
            gamma = math.exp(log_gammas[h])
            q = q_ref[r, :, h * dk:(h + 1) * dk]
            k = k_ref[r, :, h * dk:(h + 1) * dk]
            v = v_ref[r, :, h * dv:(h + 1) * dv]
            state = s_ref[r, h]
            inner = jnp.sum(q * k, axis=-1, keepdims=True) * scale
            q8 = jnp.broadcast_to(q, (V7X_SUBLANES, dk)).astype(BF16)
            o_cross = _dot(q8, state.astype(BF16))[0:1] * gamma
            o_ref[r, :, h * dv:(h + 1) * dv] = inner * v + o_cross
            k_diag = jnp.where(eye, jnp.broadcast_to(k * scale, (dk, dk)), 0.0).astype(BF16)
            v_rows = jnp.broadcast_to(v, (dk, dv)).astype(BF16)
            sn_ref[r, h] = gamma * state + _dot(k_diag, v_rows)


def retention_sample(q, k, v, state, layer, stacked, *, n_heads, dk, dv, log_gammas):
    nb = q.shape[0]
    wq, wv = n_heads * dk, n_heads * dv
    rb = _pick(nb, (4, 2, 1))
    row = lambda w: pl.BlockSpec((rb, 1, w), lambda b, *_: (b, 0, 0))
    state_blk = pl.BlockSpec((None, rb, n_heads, dk, dv), lambda b, *_: (layer, b, 0, 0, 0))
    ins = [q.reshape(nb, 1, wq), k.reshape(nb, 1, wq), v.reshape(nb, 1, wv), state]
    in_specs = [row(wq), row(wq), row(wv), state_blk]
    aliases = _stacked(in_specs, ins, stacked, n_out_before=1)
    o, s_new = pl.pallas_call(
        functools.partial(_retention_step_kernel, n_heads=n_heads, dk=dk, dv=dv,
                          log_gammas=log_gammas),
        grid=(nb // rb,),
        in_specs=in_specs,
        out_specs=(row(wv), state_blk),
        out_shape=(jax.ShapeDtypeStruct((nb, 1, wv), F32),
                   jax.ShapeDtypeStruct(state.shape, F32)),
        input_output_aliases=aliases,
        compiler_params=_params("parallel"),
        name="retention_sample",
    )(*ins)
    return o.reshape(nb, wv), s_new


def kernel(x_prompt, x_sample, mem_prompt, cache_k_diff, cache_v_diff, cache_mem_k, cache_mem_v,
           state_ret, page_table, norm_mix, w_in, w_out, da_lq1, da_lk1, da_lq2, da_lk2, da_subln,
           ret_norm, norm_x, norm_mem, wx_q, wx_kv, wx_o, norm_ff, w_up, w_down, norm_final):
    batch, seq, d_model = x_prompt.shape
    dec_batch, dec_seq, _ = x_sample.shape
    assert dec_seq == 1
    depth, n_pool, page, h_da, _, dh_da = cache_k_diff.shape
    dv_da = cache_v_diff.shape[-1]
    _, _, h_ret, dk_ret, dv_ret = state_ret.shape
    _, _, n_mem, h_x, dh_x = cache_mem_k.shape
    d_qk = h_da * 2 * dh_da
    d_da = h_da * dv_da
    d_rqk = h_ret * dk_ret
    d_ret = h_ret * dv_ret
    d_x = h_x * dh_x
    chunk = _pick(seq, (256, 128))
    mp = batch * seq

    slopes = 2.0 ** (-8.0 * (jnp.arange(h_da, dtype=F32) + 1.0) / h_da)
    log_gammas = tuple(math.log(1.0 - 2.0 ** (-5.0 - h)) for h in range(h_ret))

    o_q, o_k, o_v = 0, d_qk, 2 * d_qk
    o_qr = o_v + d_da
    o_kr, o_vr = o_qr + d_rqk, o_qr + 2 * d_rqk
    o_g = o_vr + d_ret

    cache_kt = jnp.transpose(cache_k_diff, (0, 1, 3, 4, 5, 2)).reshape(depth, n_pool, d_qk, page)
    mem_tiles = n_mem * h_x // V7X_SUBLANES
    mem_k = cache_mem_k.reshape(depth, dec_batch, mem_tiles, V7X_SUBLANES, dh_x)
    mem_v = cache_mem_v.reshape(depth, dec_batch, mem_tiles, V7X_SUBLANES, dh_x)

    xp = x_prompt.reshape(mp, d_model)
    xs = x_sample.reshape(dec_batch, d_model)
    mem = mem_prompt.reshape(batch * n_mem, d_model)

    wu_all, wd_all = w_up.astype(BF16), w_down.astype(BF16)
    outs = {name: [] for name in ("pk", "ps", "pmk", "pmv", "sk", "sv")}
    pv_all = ss_all = None
    for l in range(depth):
        lam_init = 0.8 - 0.6 * math.exp(-0.3 * l)
        lparams = tuple(a[l].reshape(1, dh_da) for a in (da_lq1, da_lk1, da_lq2, da_lk2))
        subln = da_subln[l].reshape(1, dv_da)
        w_in_l = w_in[l]
        w_a = jnp.concatenate([w_in_l[:, o_q:o_k], w_in_l[:, o_qr:o_g]], axis=1).astype(BF16)
        w_k = w_in_l[:, o_k:o_v].astype(BF16)
        w_v = w_in_l[:, o_v:o_qr].astype(BF16)
        w_g = w_in_l[:, o_g:].astype(BF16)
        w_out_da = w_out[l, :d_da].astype(BF16)
        w_out_r = w_out[l, d_da:].astype(BF16)
        wq_b, wkv_b, wo_b = wx_q[l].astype(BF16), wx_kv[l].astype(BF16), wx_o[l].astype(BF16)
        a_q, a_qr, a_kr, a_vr = 0, d_qk, d_qk + d_rqk, d_qk + 2 * d_rqk

        last = l == depth - 1
        g_next = norm_final if last else norm_mix[l + 1]

        h = rmsnorm(xp, norm_mix[l], BF16) if l == 0 else hp_next
        (pa,) = matmul([(h, w_a)], (BF16,))
        k_f32, k_b = matmul([(h, w_k)], (F32, BF16))
        pv_all, v_b = vproj(h, w_v, l, depth, pv_all, n_heads=h_da)
        (g_r,) = matmul([(h, w_g)], (F32,))
        y_da = diff_attn_prompt(pa, k_b, v_b, slopes, lparams, subln,
                                batch=batch, seq=seq, n_heads=h_da, dh=dh_da, dv=dv_da, lam_init=lam_init)
        y_r, s_p = retention_prompt(pa, g_r, ret_norm[l], batch=batch, seq=seq, n_heads=h_ret,
                                    dk=dk_ret, dv=dv_ret, chunk=chunk, log_gammas=log_gammas,
                                    q_col=a_qr, k_col=a_kr, v_col=a_vr)
        (xp,) = matmul([(y_da, w_out_da), (y_r, w_out_r)], (F32,), residual=xp)
        hm = rmsnorm(mem, norm_mem[l], BF16)
        mkv_f32, mkv_b = matmul([(hm, wkv_b)], (F32, BF16))
        xp = cross_block_prompt(xp, norm_x[l], wq_b, mkv_b, wo_b, batch=batch, seq=seq, n_mem=n_mem,
                                n_heads=h_x, dh=dh_x)
        if last:
            (y_prompt,) = mlp(xp, norm_ff[l], wu_all, wd_all, l, g_next, keep_x=False)
        else:
            xp, hp_next = mlp(xp, norm_ff[l], wu_all, wd_all, l, g_next, keep_x=True)
        outs["pk"].append(k_f32.reshape(batch, seq, h_da, 2, dh_da))
        outs["ps"].append(s_p)
        outs["pmk"].append(mkv_f32[:, :d_x].reshape(batch, n_mem, h_x, dh_x))
        outs["pmv"].append(mkv_f32[:, d_x:].reshape(batch, n_mem, h_x, dh_x))

        h = rmsnorm(xs, norm_mix[l], BF16) if l == 0 else hs_next
        (ps_a,) = matmul([(h, w_a)], (F32,))
        (k_new,) = matmul([(h, w_k)], (F32,))
        (v_new,) = matmul([(h, w_v)], (F32,))
        (g_s,) = matmul([(h, w_g)], (F32,))
        y_da = diff_attn_sample(ps_a[:, a_q:a_qr], k_new, v_new, cache_kt, cache_v_diff, l, page_table,
                                slopes, lparams, subln, n_heads=h_da, dh=dh_da, dv=dv_da,
                                lam_init=lam_init)
        o_r, ss_all = retention_sample(ps_a[:, a_qr:a_kr], ps_a[:, a_kr:a_vr], ps_a[:, a_vr:], state_ret, l, ss_all,
                                    n_heads=h_ret, dk=dk_ret, dv=dv_ret, log_gammas=log_gammas)
        y_r = gated_headnorm(o_r.reshape(dec_batch * h_ret, dv_ret), ret_norm[l],
                             g_s.reshape(dec_batch * h_ret, dv_ret))
        (xs,) = matmul([(y_da, w_out_da), (y_r.reshape(dec_batch, d_ret), w_out_r)],
                       (F32,), residual=xs)
        hx = rmsnorm(xs, norm_x[l], BF16)
        (qx,) = matmul([(hx, wq_b)], (F32,))
        ox = cross_attn_sample(qx, mem_k, mem_v, l, n_heads=h_x, dh=dh_x)
        (xs,) = matmul([(ox.astype(BF16), wo_b)], (F32,), residual=xs)
        if last:
            (y_sample,) = mlp(xs, norm_ff[l], wu_all, wd_all, l, g_next, keep_x=False)
        else:
            xs, hs_next = mlp(xs, norm_ff[l], wu_all, wd_all, l, g_next, keep_x=True)
        outs["sk"].append(k_new.reshape(dec_batch, 1, h_da, 2, dh_da))
        outs["sv"].append(v_new.reshape(dec_batch, 1, h_da, dv_da))

    y_prompt = y_prompt.reshape(batch, seq, d_model)
    y_sample = y_sample.reshape(dec_batch, 1, d_model)
    st = lambda name: jnp.stack(outs[name])
    return (y_prompt, y_sample, st("pk"), pv_all.reshape(depth, batch, seq, h_da, dv_da), st("ps"),
            st("pmk"), st("pmv"), st("sk"), st("sv"), ss_all)
```

```python
import functools
import math

import jax
import jax.numpy as jnp
from jax import lax
from jax.experimental import pallas as pl
from jax.experimental.pallas import tpu as pltpu

EPS = 1e-5
NEG_INF = -1e30
LOG2E = math.log2(math.e)
F32 = jnp.float32
BF16 = jnp.bfloat16

V7X_VMEM_LIMIT_BYTES = 60 * 1024 * 1024
V7X_LANES = 128
V7X_SUBLANES = 8


def _params(*semantics):
    return pltpu.CompilerParams(dimension_semantics=semantics,
                                vmem_limit_bytes=V7X_VMEM_LIMIT_BYTES)


def _pick(n, candidates):
    for c in candidates:
        if n % c == 0:
            return c
    return n


def _dot(a, b):
    return jnp.dot(a, b, preferred_element_type=F32)


def _dot_nt(a, b):
    return lax.dot_general(a, b, (((1,), (1,)), ((), ())), preferred_element_type=F32)


def _rms(x, g):
    ms = jnp.mean(x * x, axis=-1, keepdims=True)
    return x * lax.rsqrt(ms + EPS) * g


def _rmsnorm_kernel(x_ref, g_ref, o_ref):
    o_ref[...] = _rms(x_ref[...], g_ref[...]).astype(o_ref.dtype)


def rmsnorm(x, g, out_dtype):
    m, d = x.shape
    tm = _pick(m, (512, 256, 128))
    return pl.pallas_call(
        _rmsnorm_kernel,
        grid=(m // tm,),
        in_specs=[pl.BlockSpec((tm, d), lambda i: (i, 0)),
                  pl.BlockSpec((1, d), lambda i: (0, 0))],
        out_specs=pl.BlockSpec((tm, d), lambda i: (i, 0)),
        out_shape=jax.ShapeDtypeStruct((m, d), out_dtype),
        compiler_params=_params("parallel"),
        name="rmsnorm",
    )(x, g.reshape(1, d))


def _gated_headnorm_kernel(x_ref, g_ref, gate_ref, o_ref):
    gate = gate_ref[...]
    y = _rms(x_ref[...], g_ref[...]) * (gate * jax.nn.sigmoid(gate))
    o_ref[...] = y.astype(o_ref.dtype)


def gated_headnorm(x, g, gate):
    r, d = x.shape
    tr = _pick(r, (512, 256, 128))
    blk = pl.BlockSpec((tr, d), lambda i: (i, 0))
    return pl.pallas_call(
        _gated_headnorm_kernel,
        grid=(r // tr,),
        in_specs=[blk, pl.BlockSpec((1, d), lambda i: (0, 0)), blk],
        out_specs=blk,
        out_shape=jax.ShapeDtypeStruct((r, d), BF16),
        compiler_params=_params("parallel"),
        name="gated_headnorm",
    )(x, g.reshape(1, d), gate)


def _mm_kernel(*refs, n_pairs, has_res):
    acc = _dot(refs[0][...], refs[1][...])
    for p in range(1, n_pairs):
        acc = acc + _dot(refs[2 * p][...], refs[2 * p + 1][...])
    pos = 2 * n_pairs
    if has_res:
        acc = refs[pos][...] + acc
        pos += 1
    for o_ref in refs[pos:]:
        o_ref[...] = acc.astype(o_ref.dtype)


def matmul(pairs, out_dtypes, residual=None):
    m = pairs[0][0].shape[0]
    n = pairs[0][1].shape[1]
    tm = _pick(m, (1024, 512, 256, 128))
    tn = _pick(n, (1024, 512, 256, 128))
    ins, specs = [], []
    for a, w in pairs:
        k = a.shape[1]
        ins += [a, w]
        specs += [pl.BlockSpec((tm, k), lambda j, i: (i, 0)),
                  pl.BlockSpec((k, tn), lambda j, i: (0, j))]
    if residual is not None:
        ins.append(residual)
        specs.append(pl.BlockSpec((tm, tn), lambda j, i: (i, j)))
    return pl.pallas_call(
        functools.partial(_mm_kernel, n_pairs=len(pairs), has_res=residual is not None),
        grid=(n // tn, m // tm),
        in_specs=specs,
        out_specs=tuple(pl.BlockSpec((tm, tn), lambda j, i: (i, j)) for _ in out_dtypes),
        out_shape=tuple(jax.ShapeDtypeStruct((m, n), dt) for dt in out_dtypes),
        compiler_params=_params("parallel", "parallel"),
        name="matmul",
    )(*ins)


def _vproj_kernel(a_ref, w_ref, *rest, n_heads):
    o4_ref, ob_ref = rest[-2:]
    tm = a_ref.shape[0]
    dv = w_ref.shape[1] // n_heads
    rows = _pick(tm, (256, 128))
    for r in range(0, tm, rows):
        acc = _dot(a_ref[r:r + rows, :], w_ref[...])
        ob_ref[r:r + rows, :] = acc.astype(BF16)
        for h in range(n_heads):
            o4_ref[r:r + rows, h, :] = acc[:, h * dv:(h + 1) * dv]


def vproj(a, w, layer, depth, stacked, *, n_heads):
    m, k = a.shape
    n = w.shape[1]
    dv = n // n_heads
    tm = _pick(m, (512, 256, 128))
    ins = [a, w]
    in_specs = [pl.BlockSpec((tm, k), lambda i: (i, 0)), pl.BlockSpec((k, n), lambda i: (0, 0))]
    aliases = _stacked(in_specs, ins, stacked)
    return pl.pallas_call(
        functools.partial(_vproj_kernel, n_heads=n_heads),
        grid=(m // tm,),
        in_specs=in_specs,
        out_specs=(pl.BlockSpec((None, tm, n_heads, dv), lambda i: (layer, i, 0, 0)),
                   pl.BlockSpec((tm, n), lambda i: (i, 0))),
        out_shape=(jax.ShapeDtypeStruct((depth, m, n_heads, dv), F32),
                   jax.ShapeDtypeStruct((m, n), BF16)),
        input_output_aliases=aliases,
        compiler_params=_params("parallel"),
        name="vproj",
    )(*ins)


def _kproj_kernel(h_ref, w_ref, *rest):
    ot_ref, ob_ref = rest[-2:]
    k = _dot(h_ref[...], w_ref[...])
    ob_ref[...] = k.astype(BF16)
    ot_ref[...] = k.T


def kproj(h, w, layer, depth, stacked, *, batch, seq):
    d = h.shape[1]
    n = w.shape[1]
    ts = _pick(seq, (1024, 512, 256, 128))
    ns = seq // ts
    ins = [h, w]
    in_specs = [pl.BlockSpec((ts, d), lambda b, i: (b * ns + i, 0)),
                pl.BlockSpec((d, n), lambda b, i: (0, 0))]
    aliases = _stacked(in_specs, ins, stacked)
    return pl.pallas_call(
        _kproj_kernel,
        grid=(batch, ns),
        in_specs=in_specs,
        out_specs=(pl.BlockSpec((None, None, n, ts), lambda b, i: (layer, b, 0, i)),
                   pl.BlockSpec((ts, n), lambda b, i: (b * ns + i, 0))),
        out_shape=(jax.ShapeDtypeStruct((depth, batch, n, seq), F32),
                   jax.ShapeDtypeStruct((batch * seq, n), BF16)),
        input_output_aliases=aliases,
        compiler_params=_params("parallel", "parallel"),
        name="kproj",
    )(*ins)


def _stacked(spec_list, ins, stacked, n_out_before=0):
    if stacked is None:
        return {}
    ins.append(stacked)
    spec_list.append(pl.BlockSpec(memory_space=pl.ANY))
    return {len(ins) - 1: n_out_before}


def _mlp_kernel(x_ref, g_ref, wu_ref, wd_ref, gn_ref, *refs, keep_x):
    acc_ref, h_ref = refs[0], refs[-1]
    f = pl.program_id(1)

    @pl.when(f == 0)
    def _():
        h_ref[...] = _rms(x_ref[...], g_ref[...]).astype(BF16)

    u = _dot(h_ref[...], wu_ref[...])
    a = jnp.square(jnp.maximum(u, 0.0)).astype(BF16)
    c = _dot(a, wd_ref[...])

    @pl.when(f == 0)
    def _():
        acc_ref[...] = x_ref[...] + c

    @pl.when(f != 0)
    def _():
        acc_ref[...] += c

    @pl.when(f == pl.num_programs(1) - 1)
    def _():
        n_ref = refs[1] if keep_x else acc_ref
        n_ref[...] = _rms(acc_ref[...], gn_ref[...]).astype(n_ref.dtype)


def mlp(x, g, w_up, w_down, layer, g_next, *, keep_x):
    m, d = x.shape
    ff = w_up.shape[2]
    tm = _pick(m, (512, 256, 128))
    tf = _pick(ff, (2048, 1024, 512, 256, 128))
    row_blk = pl.BlockSpec((tm, d), lambda i, f: (i, 0))
    vec = pl.BlockSpec((1, d), lambda i, f: (0, 0))
    out_shape = [jax.ShapeDtypeStruct((m, d), F32)]
    if keep_x:
        out_shape.append(jax.ShapeDtypeStruct((m, d), BF16))
    return pl.pallas_call(
        functools.partial(_mlp_kernel, keep_x=keep_x),
        grid=(m // tm, ff // tf),
        in_specs=[row_blk, vec,
                  pl.BlockSpec((None, d, tf), lambda i, f: (layer, 0, f)),
                  pl.BlockSpec((None, tf, d), lambda i, f: (layer, f, 0)),
                  vec],
        out_specs=tuple(row_blk for _ in out_shape),
        out_shape=tuple(out_shape),
        scratch_shapes=[pltpu.VMEM((tm, d), BF16)],
        compiler_params=_params("parallel", "arbitrary"),
        name="mlp",
    )(x, g.reshape(1, d), w_up, w_down, g_next.reshape(1, d))


def _diff_lambda(lq1_ref, lk1_ref, lq2_ref, lk2_ref, lam_init):
    e1 = jnp.exp(jnp.sum(lq1_ref[...] * lk1_ref[...], axis=-1, keepdims=True))
    e2 = jnp.exp(jnp.sum(lq2_ref[...] * lk2_ref[...], axis=-1, keepdims=True))
    return e1 - e2 + lam_init


def _lane_fold(x, op):
    out = x[:, :V7X_LANES]
    for c in range(V7X_LANES, x.shape[1], V7X_LANES):
        out = op(out, x[:, c:c + V7X_LANES])
    return out


def _diff_attn_kernel(slopes_ref, q_ref, k_ref, v_ref, lq1_ref, lk1_ref, lq2_ref, lk2_ref,
                      subln_ref, o_ref, s_ref, m_ref, l_ref, acc_ref, *, t, dh, dv, group, lam_init):
    hg = pl.program_id(1)
    i = pl.program_id(2)
    kpos0 = lax.broadcasted_iota(jnp.int32, (1, t), 1).astype(F32)
    lane = lax.broadcasted_iota(jnp.int32, (t, 2 * dh), 1)

    qq, slope2 = [], []
    for g in range(group):
        q = q_ref[:, g * 2 * dh:(g + 1) * 2 * dh].astype(F32) * (dh ** -0.5 * LOG2E)
        qq.append(jnp.concatenate([jnp.where(lane < dh, q, 0.0), jnp.where(lane >= dh, q, 0.0)],
                                  axis=0).astype(BF16))
        slope2.append(slopes_ref[hg * group + g] * LOG2E)

    def scores(g, j):
        start = pl.multiple_of(j * t, t)
        kpos = kpos0 + (j * t).astype(F32)
        return _dot_nt(qq[g], k_ref[pl.ds(start, t), g * 2 * dh:(g + 1) * 2 * dh]) + slope2[g] * kpos

    def pass1(j, carry):
        for g in range(group):
            s = scores(g, j)
            s_ref[g, :, pl.ds(pl.multiple_of(j * t, t), t)] = s
            m_ref[g] = jnp.maximum(m_ref[g], _lane_fold(s, jnp.maximum))
        return carry

    row = lax.broadcasted_iota(jnp.int32, (2 * t, t), 0)
    col = lax.broadcasted_iota(jnp.int32, (2 * t, t), 1)
    visible = jnp.where(row >= t, row - t, row) >= col
    for g in range(group):
        s = jnp.where(visible, scores(g, i), NEG_INF)
        s_ref[g, :, pl.ds(pl.multiple_of(i * t, t), t)] = s
        m_ref[g] = _lane_fold(s, jnp.maximum)
    lax.fori_loop(0, i, pass1, 0)

    def pass2_block(g, j, mb, first):
        start = pl.multiple_of(j * t, t)
        ps = []
        lsum = None if first else l_ref[g]
        for c in range(0, t, V7X_LANES):
            p = jnp.exp2(s_ref[g, :, pl.ds(start + c, V7X_LANES)] - mb)
            lsum = p if lsum is None else lsum + p
            ps.append(p.astype(BF16))
        l_ref[g] = lsum
        pv = _dot(jnp.concatenate(ps, axis=1), v_ref[pl.ds(start, t), g * dv:(g + 1) * dv])
        acc_ref[g] = pv if first else acc_ref[g] + pv

    for g in range(group):
        mb = jnp.broadcast_to(jnp.max(m_ref[g], axis=-1, keepdims=True), (2 * t, V7X_LANES))
        m_ref[g] = mb
        pass2_block(g, i, mb, True)

    def pass2(j, carry):
        for g in range(group):
            pass2_block(g, j, m_ref[g], False)
        return carry

    lax.fori_loop(0, i, pass2, 0)

    lam = _diff_lambda(lq1_ref, lk1_ref, lq2_ref, lk2_ref, lam_init)
    for g in range(group):
        o = acc_ref[g] / jnp.sum(l_ref[g], axis=-1, keepdims=True)
        o = o[:t] - lam * o[t:]
        o_ref[:, g * dv:(g + 1) * dv] = (_rms(o, subln_ref[...]) * (1.0 - lam_init)).astype(o_ref.dtype)


def diff_attn_prompt(q, k, v, slopes, lparams, subln, *, batch, seq, n_heads, dh, dv, lam_init):
    t = _pick(seq, (512, 256, 128))
    nq = seq // t
    group = _pick(n_heads, (4, 2, 1))
    vec = lambda n: pl.BlockSpec((1, n), lambda b, h, i: (0, 0))
    return pl.pallas_call(
        functools.partial(_diff_attn_kernel, t=t, dh=dh, dv=dv, group=group, lam_init=lam_init),
        grid=(batch, n_heads // group, nq),
        in_specs=[pl.BlockSpec(memory_space=pltpu.SMEM),
                  pl.BlockSpec((t, group * 2 * dh), lambda b, h, i: (b * nq + i, h)),
                  pl.BlockSpec((seq, group * 2 * dh), lambda b, h, i: (b, h)),
                  pl.BlockSpec((seq, group * dv), lambda b, h, i: (b, h)),
                  vec(dh), vec(dh), vec(dh), vec(dh), vec(dv)],
        out_specs=pl.BlockSpec((t, group * dv), lambda b, h, i: (b * nq + i, h)),
        out_shape=jax.ShapeDtypeStruct((batch * seq, n_heads * dv), BF16),
        scratch_shapes=[pltpu.VMEM((group, 2 * t, seq), F32),
                        pltpu.VMEM((group, 2 * t, V7X_LANES), F32),
                        pltpu.VMEM((group, 2 * t, V7X_LANES), F32),
                        pltpu.VMEM((group, 2 * t, dv), F32)],
        compiler_params=_params("parallel", "parallel", "arbitrary"),
        name="diff_attn_prompt",
    )(slopes, q, k, v, *lparams, subln)


def _retention_kernel(q_ref, k_ref, v_ref, g_ref, norm_ref, y_ref, s_ref, *, c, n_heads, dk, dv,
                      log_gammas):
    step = pl.program_id(1)

    @pl.when(step == 0)
    def _():
        s_ref[...] = jnp.zeros_like(s_ref)

    row = lax.broadcasted_iota(jnp.int32, (c, c), 0)
    col = lax.broadcasted_iota(jnp.int32, (c, c), 1)
    diff = (row - col).astype(F32)
    n_v = lax.broadcasted_iota(jnp.int32, (c, dv), 0).astype(F32)
    n_k = lax.broadcasted_iota(jnp.int32, (c, dk), 0).astype(F32)
    scale = dk ** -0.5
    for h in range(n_heads):
        lg = log_gammas[h]
        q = q_ref[:, h * dk:(h + 1) * dk]
        k = k_ref[:, h * dk:(h + 1) * dk]
        v = v_ref[:, h * dv:(h + 1) * dv]
        state = s_ref[h]
        dmask = jnp.where(diff >= 0, jnp.exp(lg * jnp.maximum(diff, 0.0)), 0.0) * scale
        inner = _dot_nt(q, k) * dmask
        o = _dot(inner.astype(BF16), v)
        o = o + _dot(q, state.astype(BF16)) * jnp.exp(lg * (n_v + 1.0))
        k_dec = (k.astype(F32) * (jnp.exp(lg * (c - 1.0 - n_k)) * scale)).T
        s_ref[h] = math.exp(lg * c) * state + _dot(k_dec.astype(BF16), v)
        gate = g_ref[:, h * dv:(h + 1) * dv]
        y = _rms(o, norm_ref[...]) * (gate * jax.nn.sigmoid(gate))
        y_ref[:, h * dv:(h + 1) * dv] = y.astype(y_ref.dtype)


def retention_prompt(pa, g_r, ret_norm, *, batch, seq, n_heads, dk, dv, chunk, log_gammas,
                     q_col, k_col, v_col):
    nc = seq // chunk
    wq = n_heads * dk
    wv = n_heads * dv
    return pl.pallas_call(
        functools.partial(_retention_kernel, c=chunk, n_heads=n_heads, dk=dk, dv=dv,
                          log_gammas=log_gammas),
        grid=(batch, nc),
        in_specs=[pl.BlockSpec((chunk, wq), lambda b, s: (b * nc + s, q_col // wq)),
                  pl.BlockSpec((chunk, wq), lambda b, s: (b * nc + s, k_col // wq)),
                  pl.BlockSpec((chunk, wv), lambda b, s: (b * nc + s, v_col // wv)),
                  pl.BlockSpec((chunk, wv), lambda b, s: (b * nc + s, 0)),
                  pl.BlockSpec((1, dv), lambda b, s: (0, 0))],
        out_specs=(pl.BlockSpec((chunk, wv), lambda b, s: (b * nc + s, 0)),
                   pl.BlockSpec((None, n_heads, dk, dv), lambda b, s: (b, 0, 0, 0))),
        out_shape=(jax.ShapeDtypeStruct((batch * seq, wv), BF16),
                   jax.ShapeDtypeStruct((batch, n_heads, dk, dv), F32)),
        compiler_params=_params("parallel", "arbitrary"),
        name="retention_prompt",
    )(pa, pa, pa, g_r, ret_norm.reshape(1, dv))


def _cross_block_kernel(x_ref, g_ref, wq_ref, k_ref, v_ref, wo_ref, o_ref, *, n_heads, dh):
    x = x_ref[...]
    q = _dot(_rms(x, g_ref[...]).astype(BF16), wq_ref[...]).astype(BF16)
    heads = []
    for h in range(n_heads):
        sl = slice(h * dh, (h + 1) * dh)
        s = _dot_nt(q[:, sl], k_ref[:, sl]) * (dh ** -0.5)
        m = jnp.max(s, axis=-1, keepdims=True)
        p = jnp.exp(s - m)
        p = p / jnp.sum(p, axis=-1, keepdims=True)
        heads.append(_dot(p.astype(BF16), v_ref[:, sl]).astype(BF16))
    o_ref[...] = x + _dot(jnp.concatenate(heads, axis=1), wo_ref[...])


def cross_block_prompt(x, g, wq, mkv, wo, *, batch, seq, n_mem, n_heads, dh):
    d = x.shape[1]
    w = n_heads * dh
    tq = _pick(seq, (512, 256, 128))
    nq = seq // tq
    row_blk = pl.BlockSpec((tq, d), lambda b, i: (b * nq + i, 0))
    return pl.pallas_call(
        functools.partial(_cross_block_kernel, n_heads=n_heads, dh=dh),
        grid=(batch, nq),
        in_specs=[row_blk,
                  pl.BlockSpec((1, d), lambda b, i: (0, 0)),
                  pl.BlockSpec((d, w), lambda b, i: (0, 0)),
                  pl.BlockSpec((n_mem, w), lambda b, i: (b, 0)),
                  pl.BlockSpec((n_mem, w), lambda b, i: (b, 1)),
                  pl.BlockSpec((w, d), lambda b, i: (0, 0))],
        out_specs=row_blk,
        out_shape=jax.ShapeDtypeStruct(x.shape, F32),
        compiler_params=_params("parallel", "parallel"),
        name="cross_block_prompt",
    )(x, g.reshape(1, d), wq, mkv, mkv, wo)


def _diff_decode_kernel(*refs, n_pages, page, n_heads, dh, lam_init):
    pos = 1
    qt_ref, knt_ref, vn_ref = refs[pos:pos + 3]; pos += 3
    kt_refs = refs[pos:pos + n_pages]; pos += n_pages
    v_refs = refs[pos:pos + n_pages]; pos += n_pages
    slopes_ref = refs[pos]; pos += 1
    lparams = refs[pos:pos + 4]; pos += 4
    subln_ref = refs[pos]; pos += 1
    o_ref = refs[pos]

    b = pl.program_id(0)
    nb = qt_ref.shape[1]
    onehot = (lax.broadcasted_iota(jnp.int32, (nb, V7X_LANES), 0) == b).astype(BF16)
    q_rep = _dot(qt_ref[...], onehot) * (dh ** -0.5)
    kn_rep = _dot(knt_ref[...], onehot)

    def map_sums(x):
        x3 = x.reshape(n_heads, 2 * dh, x.shape[-1])
        return x3[:, :dh, :].sum(axis=1), x3[:, dh:, :].sum(axis=1)

    past = n_pages * page
    tpos = lax.broadcasted_iota(jnp.int32, (n_heads, page), 1).astype(F32)
    nslope = -slopes_ref[...]
    s1, s2 = [], []
    for p in range(n_pages):
        a, c = map_sums(kt_refs[p][...] * q_rep)
        bias = nslope * ((past - p * page) - tpos)
        s1.append(a + bias)
        s2.append(c + bias)
    n1, n2 = map_sums(kn_rep * q_rep)

    def softmax_parts(s_list, s_new):
        m = s_list[0]
        for s in s_list[1:]:
            m = jnp.maximum(m, s)
        m = jnp.maximum(jnp.max(m, axis=-1, keepdims=True), s_new)
        e = [jnp.exp(s - m) for s in s_list]
        e_new = jnp.exp(s_new - m)
        l = e[0]
        for x in e[1:]:
            l = l + x
        l = jnp.sum(l, axis=-1, keepdims=True) + e_new
        return e, e_new, 1.0 / l

    e1, en1, a1 = softmax_parts(s1, n1)
    e2, en2, a2 = softmax_parts(s2, n2)
    a2 = a2 * _diff_lambda(*lparams, lam_init)

    tok = lax.broadcasted_iota(jnp.int32, (page, n_heads, V7X_LANES), 0)
    lane = lax.broadcasted_iota(jnp.int32, (page, n_heads, V7X_LANES), 2)
    diag = tok == lane
    ones = jnp.ones((V7X_LANES, V7X_LANES), BF16)
    o = (en1 * a1 - en2 * a2) * vn_ref[...]
    for p in range(n_pages):
        w = e1[p] * a1 - e2[p] * a2
        wd = jnp.where(diag, w[None], 0.0).reshape(page * n_heads, V7X_LANES).astype(BF16)
        w3 = _dot(wd, ones).reshape(page, n_heads, V7X_LANES)
        o = o + jnp.sum(w3 * v_refs[p][...], axis=0)
    o_ref[...] = (_rms(o, subln_ref[...]) * (1.0 - lam_init)).astype(o_ref.dtype)


def diff_attn_sample(q, k_new, v_new, cache_kt, cache_v, layer, page_table, slopes, lparams, subln, *,
                     n_heads, dh, dv, lam_init):
    nb, n_pages = page_table.shape
    page = cache_v.shape[2]
    wk = n_heads * 2 * dh
    assert page == V7X_LANES and dv == V7X_LANES and n_heads == V7X_SUBLANES
    const = lambda shape: pl.BlockSpec(shape, lambda b, pt: (0,) * len(shape))

    def kt_spec(p):
        return pl.BlockSpec((None, None, wk, page), lambda b, pt: (layer, pt[b, p], 0, 0))

    def v_spec(p):
        return pl.BlockSpec((None, None, page, n_heads, dv), lambda b, pt: (layer, pt[b, p], 0, 0, 0))

    in_specs = ([const((wk, nb)), const((wk, nb)),
                 pl.BlockSpec((None, n_heads, dv), lambda b, pt: (b, 0, 0))]
                + [kt_spec(p) for p in range(n_pages)] + [v_spec(p) for p in range(n_pages)]
                + [const((n_heads, V7X_LANES))] + [const((1, dh))] * 4 + [const((1, dv))])
    out = pl.pallas_call(
        functools.partial(_diff_decode_kernel, n_pages=n_pages, page=page, n_heads=n_heads, dh=dh,
                          lam_init=lam_init),
        grid_spec=pltpu.PrefetchScalarGridSpec(
            num_scalar_prefetch=1, grid=(nb,), in_specs=in_specs,
            out_specs=pl.BlockSpec((None, n_heads, dv), lambda b, pt: (b, 0, 0))),
        out_shape=jax.ShapeDtypeStruct((nb, n_heads, dv), BF16),
        compiler_params=_params("arbitrary"),
        name="diff_attn_sample",
    )(page_table, q.T.astype(BF16), k_new.T.astype(BF16), v_new.reshape(nb, n_heads, dv),
      *([cache_kt] * n_pages), *([cache_v] * n_pages),
      jnp.broadcast_to(slopes[:, None], (n_heads, V7X_LANES)), *lparams, subln)
    return out.reshape(nb, n_heads * dv)


def _cross_decode_kernel(q_ref, k_ref, v_ref, o_ref, *, n_heads, dh):
    n_req, n_tiles = k_ref.shape[:2]
    ones = jnp.ones((dh, V7X_LANES), BF16)

    def over_tokens(x, op):
        out = x
        for r in range(n_heads, V7X_SUBLANES, n_heads):
            out = op(out, pltpu.roll(x, r, axis=0))
        return out

    for r in range(n_req):
        q = q_ref[r] * (dh ** -0.5)
        prod = (k_ref[r] * q[None]).reshape(n_tiles * V7X_SUBLANES, dh).astype(BF16)
        s = _dot(prod, ones).reshape(n_tiles, V7X_SUBLANES, V7X_LANES)
        m = over_tokens(jnp.max(s, axis=0), jnp.maximum)
        e = jnp.exp(s - m[None])
        l = over_tokens(jnp.sum(e, axis=0), jnp.add)
        o = over_tokens(jnp.sum(e * v_ref[r], axis=0), jnp.add)
        o_ref[r] = o / l


def cross_attn_sample(q, cache_k, cache_v, layer, *, n_heads, dh):
    nb = q.shape[0]
    assert dh == V7X_LANES and V7X_SUBLANES % n_heads == 0
    n_tiles = cache_k.shape[2]
    q8 = jnp.tile(q.reshape(nb, n_heads, dh), (1, V7X_SUBLANES // n_heads, 1))
    rb = _pick(nb, (4, 2, 1))
    kv_spec = pl.BlockSpec((None, rb, n_tiles, V7X_SUBLANES, dh), lambda b: (layer, b, 0, 0, 0))
    row = pl.BlockSpec((rb, V7X_SUBLANES, dh), lambda b: (b, 0, 0))
    out = pl.pallas_call(
        functools.partial(_cross_decode_kernel, n_heads=n_heads, dh=dh),
        grid=(nb // rb,),
        in_specs=[row, kv_spec, kv_spec],
        out_specs=row,
        out_shape=jax.ShapeDtypeStruct((nb, V7X_SUBLANES, dh), F32),
        compiler_params=_params("parallel"),
        name="cross_attn_sample",
    )(q8, cache_k, cache_v)
    return out[:, :n_heads].reshape(nb, n_heads * dh)


def _retention_step_kernel(q_ref, k_ref, v_ref, s_ref, *rest, n_heads, dk, dv, log_gammas):
    o_ref, sn_ref = rest[-2:]
    scale = dk ** -0.5
    eye = (lax.broadcasted_iota(jnp.int32, (dk, dk), 0)
           == lax.broadcasted_iota(jnp.int32, (dk, dk), 1))
    for r in range(q_ref.shape[0]):
        for h in range(n_heads):
            gamma = math.exp(log_gammas[h])
            q = q_ref[r, :, h * dk:(h + 1) * dk]
            k = k_ref[r, :, h * dk:(h + 1) * dk]
            v = v_ref[r, :, h * dv:(h + 1) * dv]
            state = s_ref[r, h]
            inner = jnp.sum(q * k, axis=-1, keepdims=True) * scale
            q8 = jnp.broadcast_to(q, (V7X_SUBLANES, dk)).astype(BF16)
            o_cross = _dot(q8, state.astype(BF16))[0:1] * gamma
            o_ref[r, :, h * dv:(h + 1) * dv] = inner * v + o_cross
            k_diag = jnp.where(eye, jnp.broadcast_to(k * scale, (dk, dk)), 0.0).astype(BF16)
            v_rows = jnp.broadcast_to(v, (dk, dv)).astype(BF16)
            sn_ref[r, h] = gamma * state + _dot(k_diag, v_rows)


def retention_sample(q, k, v, state, layer, stacked, *, n_heads, dk, dv, log_gammas):
    nb = q.shape[0]
    wq, wv = n_heads * dk, n_heads * dv
    rb = _pick(nb, (4, 2, 1))
    row = lambda w: pl.BlockSpec((rb, 1, w), lambda b, *_: (b, 0, 0))
    state_blk = pl.BlockSpec((None, rb, n_heads, dk, dv), lambda b, *_: (layer, b, 0, 0, 0))
    ins = [q.reshape(nb, 1, wq), k.reshape(nb, 1, wq), v.reshape(nb, 1, wv), state]
    in_specs = [row(wq), row(wq), row(wv), state_blk]
    aliases = _stacked(in_specs, ins, stacked, n_out_before=1)
    o, s_new = pl.pallas_call(
        functools.partial(_retention_step_kernel, n_heads=n_heads, dk=dk, dv=dv,
                          log_gammas=log_gammas),
        grid=(nb // rb,),
        in_specs=in_specs,
        out_specs=(row(wv), state_blk),
        out_shape=(jax.ShapeDtypeStruct((nb, 1, wv), F32),
                   jax.ShapeDtypeStruct(state.shape, F32)),
        input_output_aliases=aliases,
        compiler_params=_params("parallel"),
        name="retention_sample",
    )(*ins)
    return o.reshape(nb, wv), s_new


def kernel(x_prompt, x_sample, mem_prompt, cache_k_diff, cache_v_diff, cache_mem_k, cache_mem_v,
           state_ret, page_table, norm_mix, w_in, w_out, da_lq1, da_lk1, da_lq2, da_lk2, da_subln,
           ret_norm, norm_x, norm_mem, wx_q, wx_kv, wx_o, norm_ff, w_up, w_down, norm_final):
    batch, seq, d_model = x_prompt.shape
    dec_batch, dec_seq, _ = x_sample.shape
    assert dec_seq == 1
    depth, n_pool, page, h_da, _, dh_da = cache_k_diff.shape
    dv_da = cache_v_diff.shape[-1]
    _, _, h_ret, dk_ret, dv_ret = state_ret.shape
    _, _, n_mem, h_x, dh_x = cache_mem_k.shape
    d_qk = h_da * 2 * dh_da
    d_da = h_da * dv_da
    d_rqk = h_ret * dk_ret
    d_ret = h_ret * dv_ret
    d_x = h_x * dh_x
    chunk = _pick(seq, (256, 128))
    mp = batch * seq

    slopes = 2.0 ** (-8.0 * (jnp.arange(h_da, dtype=F32) + 1.0) / h_da)
    log_gammas = tuple(math.log(1.0 - 2.0 ** (-5.0 - h)) for h in range(h_ret))

    o_q, o_k, o_v = 0, d_qk, 2 * d_qk
    o_qr = o_v + d_da
    o_kr, o_vr = o_qr + d_rqk, o_qr + 2 * d_rqk
    o_g = o_vr + d_ret

    cache_kt = jnp.transpose(cache_k_diff, (0, 1, 3, 4, 5, 2)).reshape(depth, n_pool, d_qk, page)
    mem_tiles = n_mem * h_x // V7X_SUBLANES
    mem_k = cache_mem_k.reshape(depth, dec_batch, mem_tiles, V7X_SUBLANES, dh_x)
    mem_v = cache_mem_v.reshape(depth, dec_batch, mem_tiles, V7X_SUBLANES, dh_x)

    xp = x_prompt.reshape(mp, d_model)
    xs = x_sample.reshape(dec_batch, d_model)
    mem = mem_prompt.reshape(batch * n_mem, d_model)

    wu_all, wd_all = w_up.astype(BF16), w_down.astype(BF16)
    outs = {name: [] for name in ("ps", "pmk", "pmv", "sk", "sv")}
    pk_all = pv_all = ss_all = None
    for l in range(depth):
        lam_init = 0.8 - 0.6 * math.exp(-0.3 * l)
        lparams = tuple(a[l].reshape(1, dh_da) for a in (da_lq1, da_lk1, da_lq2, da_lk2))
        subln = da_subln[l].reshape(1, dv_da)
        w_in_l = w_in[l]
        w_a = jnp.concatenate([w_in_l[:, o_q:o_k], w_in_l[:, o_qr:o_g]], axis=1).astype(BF16)
        w_k = w_in_l[:, o_k:o_v].astype(BF16)
        w_v = w_in_l[:, o_v:o_qr].astype(BF16)
        w_g = w_in_l[:, o_g:].astype(BF16)
        w_out_da = w_out[l, :d_da].astype(BF16)
        w_out_r = w_out[l, d_da:].astype(BF16)
        wq_b, wkv_b, wo_b = wx_q[l].astype(BF16), wx_kv[l].astype(BF16), wx_o[l].astype(BF16)
        a_q, a_qr, a_kr, a_vr = 0, d_qk, d_qk + d_rqk, d_qk + 2 * d_rqk

        last = l == depth - 1
        g_next = norm_final if last else norm_mix[l + 1]

        h = rmsnorm(xp, norm_mix[l], BF16) if l == 0 else hp_next
        (pa,) = matmul([(h, w_a)], (BF16,))
        pk_all, k_b = kproj(h, w_k, l, depth, pk_all, batch=batch, seq=seq)
        pv_all, v_b = vproj(h, w_v, l, depth, pv_all, n_heads=h_da)
        (g_r,) = matmul([(h, w_g)], (F32,))
        y_da = diff_attn_prompt(pa, k_b, v_b, slopes, lparams, subln,
                                batch=batch, seq=seq, n_heads=h_da, dh=dh_da, dv=dv_da, lam_init=lam_init)
        y_r, s_p = retention_prompt(pa, g_r, ret_norm[l], batch=batch, seq=seq, n_heads=h_ret,
                                    dk=dk_ret, dv=dv_ret, chunk=chunk, log_gammas=log_gammas,
                                    q_col=a_qr, k_col=a_kr, v_col=a_vr)
        (xp,) = matmul([(y_da, w_out_da), (y_r, w_out_r)], (F32,), residual=xp)
        hm = rmsnorm(mem, norm_mem[l], BF16)
        mkv_f32, mkv_b = matmul([(hm, wkv_b)], (F32, BF16))
        xp = cross_block_prompt(xp, norm_x[l], wq_b, mkv_b, wo_b, batch=batch, seq=seq, n_mem=n_mem,
                                n_heads=h_x, dh=dh_x)
        if last:
            (y_prompt,) = mlp(xp, norm_ff[l], wu_all, wd_all, l, g_next, keep_x=False)
        else:
            xp, hp_next = mlp(xp, norm_ff[l], wu_all, wd_all, l, g_next, keep_x=True)
        outs["ps"].append(s_p)
        outs["pmk"].append(mkv_f32[:, :d_x].reshape(batch, n_mem, h_x, dh_x))
        outs["pmv"].append(mkv_f32[:, d_x:].reshape(batch, n_mem, h_x, dh_x))

        h = rmsnorm(xs, norm_mix[l], BF16) if l == 0 else hs_next
        (ps_a,) = matmul([(h, w_a)], (F32,))
        (k_new,) = matmul([(h, w_k)], (F32,))
        (v_new,) = matmul([(h, w_v)], (F32,))
        (g_s,) = matmul([(h, w_g)], (F32,))
        y_da = diff_attn_sample(ps_a[:, a_q:a_qr], k_new, v_new, cache_kt, cache_v_diff, l, page_table,
                                slopes, lparams, subln, n_heads=h_da, dh=dh_da, dv=dv_da,
                                lam_init=lam_init)
        o_r, ss_all = retention_sample(ps_a[:, a_qr:a_kr], ps_a[:, a_kr:a_vr], ps_a[:, a_vr:], state_ret, l, ss_all,
                                    n_heads=h_ret, dk=dk_ret, dv=dv_ret, log_gammas=log_gammas)
        y_r = gated_headnorm(o_r.reshape(dec_batch * h_ret, dv_ret), ret_norm[l],
                             g_s.reshape(dec_batch * h_ret, dv_ret))
        (xs,) = matmul([(y_da, w_out_da), (y_r.reshape(dec_batch, d_ret), w_out_r)],
                       (F32,), residual=xs)
        hx = rmsnorm(xs, norm_x[l], BF16)
        (qx,) = matmul([(hx, wq_b)], (F32,))
        ox = cross_attn_sample(qx, mem_k, mem_v, l, n_heads=h_x, dh=dh_x)
        (xs,) = matmul([(ox.astype(BF16), wo_b)], (F32,), residual=xs)
        if last:
            (y_sample,) = mlp(xs, norm_ff[l], wu_all, wd_all, l, g_next, keep_x=False)
        else:
            xs, hs_next = mlp(xs, norm_ff[l], wu_all, wd_all, l, g_next, keep_x=True)
        outs["sk"].append(k_new.reshape(dec_batch, 1, h_da, 2, dh_da))
        outs["sv"].append(v_new.reshape(dec_batch, 1, h_da, dv_da))

    y_prompt = y_prompt.reshape(batch, seq, d_model)
    y_sample = y_sample.reshape(dec_batch, 1, d_model)
    st = lambda name: jnp.stack(outs[name])
    new_k_prompt = jnp.transpose(pk_all.reshape(depth, batch, h_da, 2, dh_da, seq), (0, 1, 5, 2, 3, 4))
    return (y_prompt, y_sample, new_k_prompt, pv_all.reshape(depth, batch, seq, h_da, dv_da), st("ps"),
            st("pmk"), st("pmv"), st("sk"), st("sv"), ss_all)
```

```python
import functools
import math
from typing import NamedTuple

import jax
import jax.numpy as jnp
from jax import lax
from jax.experimental import pallas as pl
from jax.experimental.pallas import tpu as pltpu

EPS = 1e-5
NEG_INF = -1e30
LOG2E = math.log2(math.e)
F32 = jnp.float32
BF16 = jnp.bfloat16

V7X_VMEM_LIMIT_BYTES = 60 * 1024 * 1024
V7X_LANES = 128
V7X_SUBLANES = 8


def _params(*semantics):
    return pltpu.CompilerParams(dimension_semantics=semantics,
                                vmem_limit_bytes=V7X_VMEM_LIMIT_BYTES)


def _pick(n, candidates):
    for c in candidates:
        if n % c == 0:
            return c
    return n


def _dot(a, b):
    return jnp.dot(a, b, preferred_element_type=F32)


def _dot_nt(a, b):
    return lax.dot_general(a, b, (((1,), (1,)), ((), ())), preferred_element_type=F32)


class WView(NamedTuple):
    arr: jax.Array
    layer: int
    row0: int
    k: int
    segs: tuple

    @property
    def n(self):
        return sum(width for _, width in self.segs)

    def spec(self, tn, grid_rank):
        assert self.row0 % self.k == 0 and all(c % tn == 0 and w % tn == 0 for c, w in self.segs)
        blocks = [c // tn + t for c, w in self.segs for t in range(w // tn)]
        layer, row_blk = self.layer, self.row0 // self.k

        def col(j):
            blk = blocks[-1]
            for idx in range(len(blocks) - 2, -1, -1):
                blk = jnp.where(j == idx, blocks[idx], blk)
            return blk

        if len(blocks) == 1:
            return pl.BlockSpec((None, self.k, tn), lambda *g: (layer, row_blk, blocks[0]))
        assert grid_rank == 2
        return pl.BlockSpec((None, self.k, tn), lambda j, i: (layer, row_blk, col(j)))


def _rms(x, g):
    ms = jnp.mean(x * x, axis=-1, keepdims=True)
    return x * lax.rsqrt(ms + EPS) * g


def _rmsnorm_kernel(x_ref, g_ref, o_ref):
    o_ref[...] = _rms(x_ref[...], g_ref[...]).astype(o_ref.dtype)


def rmsnorm(x, g, out_dtype):
    m, d = x.shape
    tm = _pick(m, (512, 256, 128))
    return pl.pallas_call(
        _rmsnorm_kernel,
        grid=(m // tm,),
        in_specs=[pl.BlockSpec((tm, d), lambda i: (i, 0)),
                  pl.BlockSpec((1, d), lambda i: (0, 0))],
        out_specs=pl.BlockSpec((tm, d), lambda i: (i, 0)),
        out_shape=jax.ShapeDtypeStruct((m, d), out_dtype),
        compiler_params=_params("parallel"),
        name="rmsnorm",
    )(x, g.reshape(1, d))


def _gated_headnorm_kernel(x_ref, g_ref, gate_ref, o_ref):
    gate = gate_ref[...]
    y = _rms(x_ref[...], g_ref[...]) * (gate * jax.nn.sigmoid(gate))
    o_ref[...] = y.astype(o_ref.dtype)


def gated_headnorm(x, g, gate):
    r, d = x.shape
    tr = _pick(r, (512, 256, 128))
    blk = pl.BlockSpec((tr, d), lambda i: (i, 0))
    return pl.pallas_call(
        _gated_headnorm_kernel,
        grid=(r // tr,),
        in_specs=[blk, pl.BlockSpec((1, d), lambda i: (0, 0)), blk],
        out_specs=blk,
        out_shape=jax.ShapeDtypeStruct((r, d), BF16),
        compiler_params=_params("parallel"),
        name="gated_headnorm",
    )(x, g.reshape(1, d), gate)


def _mm_kernel(*refs, n_pairs, has_res):
    acc = _dot(refs[0][...], refs[1][...].astype(BF16))
    for p in range(1, n_pairs):
        acc = acc + _dot(refs[2 * p][...], refs[2 * p + 1][...].astype(BF16))
    pos = 2 * n_pairs
    if has_res:
        acc = refs[pos][...] + acc
        pos += 1
    for o_ref in refs[pos:]:
        o_ref[...] = acc.astype(o_ref.dtype)


def matmul(pairs, out_dtypes, residual=None):
    m = pairs[0][0].shape[0]
    n = pairs[0][1].n
    tm = _pick(m, (1024, 512, 256, 128))
    tn = _pick(n, (1024, 512, 256, 128))
    ins, specs = [], []
    for a, w in pairs:
        assert a.shape[1] == w.k and w.n == n
        ins += [a, w.arr]
        specs += [pl.BlockSpec((tm, w.k), lambda j, i: (i, 0)), w.spec(tn, 2)]
    if residual is not None:
        ins.append(residual)
        specs.append(pl.BlockSpec((tm, tn), lambda j, i: (i, j)))
    return pl.pallas_call(
        functools.partial(_mm_kernel, n_pairs=len(pairs), has_res=residual is not None),
        grid=(n // tn, m // tm),
        in_specs=specs,
        out_specs=tuple(pl.BlockSpec((tm, tn), lambda j, i: (i, j)) for _ in out_dtypes),
        out_shape=tuple(jax.ShapeDtypeStruct((m, n), dt) for dt in out_dtypes),
        compiler_params=_params("parallel", "parallel"),
        name="matmul",
    )(*ins)


def _vproj_kernel(a_ref, w_ref, *rest, n_heads):
    o4_ref, ob_ref = rest[-2:]
    acc = _dot(a_ref[...], w_ref[...].astype(BF16))
    ob_ref[...] = acc.astype(BF16)
    dv = acc.shape[1] // n_heads
    for h in range(n_heads):
        o4_ref[:, h, :] = acc[:, h * dv:(h + 1) * dv]


def vproj(a, w, layer, depth, stacked, *, n_heads):
    m, k = a.shape
    n = w.n
    dv = n // n_heads
    tm = _pick(m, (512, 256, 128))
    ins = [a, w.arr]
    in_specs = [pl.BlockSpec((tm, k), lambda i: (i, 0)), w.spec(n, 1)]
    aliases = _stacked(in_specs, ins, stacked)
    return pl.pallas_call(
        functools.partial(_vproj_kernel, n_heads=n_heads),
        grid=(m // tm,),
        in_specs=in_specs,
        out_specs=(pl.BlockSpec((None, tm, n_heads, dv), lambda i: (layer, i, 0, 0)),
                   pl.BlockSpec((tm, n), lambda i: (i, 0))),
        out_shape=(jax.ShapeDtypeStruct((depth, m, n_heads, dv), F32),
                   jax.ShapeDtypeStruct((m, n), BF16)),
        input_output_aliases=aliases,
        compiler_params=_params("parallel"),
        name="vproj",
    )(*ins)


def _kproj_kernel(h_ref, w_ref, *rest):
    ot_ref, ob_ref = rest[-2:]
    k = _dot(h_ref[...], w_ref[...].astype(BF16))
    ob_ref[...] = k.astype(BF16)
    ot_ref[...] = k.T


def kproj(h, w, layer, depth, stacked, *, batch, seq):
    d = h.shape[1]
    n = w.n
    ts = _pick(seq, (1024, 512, 256, 128))
    ns = seq // ts
    ins = [h, w.arr]
    in_specs = [pl.BlockSpec((ts, d), lambda b, i: (b * ns + i, 0)), w.spec(n, 1)]
    aliases = _stacked(in_specs, ins, stacked)
    return pl.pallas_call(
        _kproj_kernel,
        grid=(batch, ns),
        in_specs=in_specs,
        out_specs=(pl.BlockSpec((None, None, n, ts), lambda b, i: (layer, b, 0, i)),
                   pl.BlockSpec((ts, n), lambda b, i: (b * ns + i, 0))),
        out_shape=(jax.ShapeDtypeStruct((depth, batch, n, seq), F32),
                   jax.ShapeDtypeStruct((batch * seq, n), BF16)),
        input_output_aliases=aliases,
        compiler_params=_params("parallel", "parallel"),
        name="kproj",
    )(*ins)


def _stacked(spec_list, ins, stacked, n_out_before=0):
    if stacked is None:
        return {}
    ins.append(stacked)
    spec_list.append(pl.BlockSpec(memory_space=pl.ANY))
    return {len(ins) - 1: n_out_before}


def _mlp_kernel(x_ref, g_ref, wu_ref, wd_ref, gn_ref, *refs, keep_x):
    acc_ref, h_ref = refs[0], refs[-1]
    f = pl.program_id(1)

    @pl.when(f == 0)
    def _():
        h_ref[...] = _rms(x_ref[...], g_ref[...]).astype(BF16)

    u = _dot(h_ref[...], wu_ref[...])
    a = jnp.square(jnp.maximum(u, 0.0)).astype(BF16)
    c = _dot(a, wd_ref[...])

    @pl.when(f == 0)
    def _():
        acc_ref[...] = x_ref[...] + c

    @pl.when(f != 0)
    def _():
        acc_ref[...] += c

    @pl.when(f == pl.num_programs(1) - 1)
    def _():
        n_ref = refs[1] if keep_x else acc_ref
        n_ref[...] = _rms(acc_ref[...], gn_ref[...]).astype(n_ref.dtype)


def mlp(x, g, w_up, w_down, layer, g_next, *, keep_x):
    m, d = x.shape
    ff = w_up.shape[2]
    tm = _pick(m, (512, 256, 128))
    tf = _pick(ff, (2048, 1024, 512, 256, 128))
    row_blk = pl.BlockSpec((tm, d), lambda i, f: (i, 0))
    vec = pl.BlockSpec((1, d), lambda i, f: (0, 0))
    out_shape = [jax.ShapeDtypeStruct((m, d), F32)]
    if keep_x:
        out_shape.append(jax.ShapeDtypeStruct((m, d), BF16))
    return pl.pallas_call(
        functools.partial(_mlp_kernel, keep_x=keep_x),
        grid=(m // tm, ff // tf),
        in_specs=[row_blk, vec,
                  pl.BlockSpec((None, d, tf), lambda i, f: (layer, 0, f)),
                  pl.BlockSpec((None, tf, d), lambda i, f: (layer, f, 0)),
                  vec],
        out_specs=tuple(row_blk for _ in out_shape),
        out_shape=tuple(out_shape),
        scratch_shapes=[pltpu.VMEM((tm, d), BF16)],
        compiler_params=_params("parallel", "arbitrary"),
        name="mlp",
    )(x, g.reshape(1, d), w_up, w_down, g_next.reshape(1, d))


def _diff_lambda(lq1_ref, lk1_ref, lq2_ref, lk2_ref, lam_init):
    e1 = jnp.exp(jnp.sum(lq1_ref[...] * lk1_ref[...], axis=-1, keepdims=True))
    e2 = jnp.exp(jnp.sum(lq2_ref[...] * lk2_ref[...], axis=-1, keepdims=True))
    return e1 - e2 + lam_init


def _lane_fold(x, op):
    out = x[:, :V7X_LANES]
    for c in range(V7X_LANES, x.shape[1], V7X_LANES):
        out = op(out, x[:, c:c + V7X_LANES])
    return out


def _diff_attn_kernel(slopes_ref, q_ref, k_ref, v_ref, lq1_ref, lk1_ref, lq2_ref, lk2_ref,
                      subln_ref, o_ref, s_ref, m_ref, l_ref, acc_ref, *, t, dh, dv, group, lam_init):
    hg = pl.program_id(1)
    i = pl.program_id(2)
    kpos0 = lax.broadcasted_iota(jnp.int32, (1, t), 1).astype(F32)
    lane = lax.broadcasted_iota(jnp.int32, (t, 2 * dh), 1)

    qq, slope2 = [], []
    for g in range(group):
        q = q_ref[:, g * 2 * dh:(g + 1) * 2 * dh].astype(F32) * (dh ** -0.5 * LOG2E)
        qq.append(jnp.concatenate([jnp.where(lane < dh, q, 0.0), jnp.where(lane >= dh, q, 0.0)],
                                  axis=0).astype(BF16))
        slope2.append(slopes_ref[hg * group + g] * LOG2E)

    def scores(g, j):
        start = pl.multiple_of(j * t, t)
        kpos = kpos0 + (j * t).astype(F32)
        return _dot_nt(qq[g], k_ref[pl.ds(start, t), g * 2 * dh:(g + 1) * 2 * dh]) + slope2[g] * kpos

    def pass1(j, carry):
        for g in range(group):
            s = scores(g, j)
            s_ref[g, :, pl.ds(pl.multiple_of(j * t, t), t)] = s
            m_ref[g] = jnp.maximum(m_ref[g], _lane_fold(s, jnp.maximum))
        return carry

    row = lax.broadcasted_iota(jnp.int32, (2 * t, t), 0)
    col = lax.broadcasted_iota(jnp.int32, (2 * t, t), 1)
    visible = jnp.where(row >= t, row - t, row) >= col
    for g in range(group):
        s = jnp.where(visible, scores(g, i), NEG_INF)
        s_ref[g, :, pl.ds(pl.multiple_of(i * t, t), t)] = s
        m_ref[g] = _lane_fold(s, jnp.maximum)
    lax.fori_loop(0, i, pass1, 0)

    def pass2_block(g, j, mb, first):
        start = pl.multiple_of(j * t, t)
        ps = []
        lsum = None if first else l_ref[g]
        for c in range(0, t, V7X_LANES):
            p = jnp.exp2(s_ref[g, :, pl.ds(start + c, V7X_LANES)] - mb)
            lsum = p if lsum is None else lsum + p
            ps.append(p.astype(BF16))
        l_ref[g] = lsum
        pv = _dot(jnp.concatenate(ps, axis=1), v_ref[pl.ds(start, t), g * dv:(g + 1) * dv])
        acc_ref[g] = pv if first else acc_ref[g] + pv

    for g in range(group):
        mb = jnp.broadcast_to(jnp.max(m_ref[g], axis=-1, keepdims=True), (2 * t, V7X_LANES))
        m_ref[g] = mb
        pass2_block(g, i, mb, True)

    def pass2(j, carry):
        for g in range(group):
            pass2_block(g, j, m_ref[g], False)
        return carry

    lax.fori_loop(0, i, pass2, 0)

    lam = _diff_lambda(lq1_ref, lk1_ref, lq2_ref, lk2_ref, lam_init)
    for g in range(group):
        o = acc_ref[g] / jnp.sum(l_ref[g], axis=-1, keepdims=True)
        o = o[:t] - lam * o[t:]
        o_ref[:, g * dv:(g + 1) * dv] = (_rms(o, subln_ref[...]) * (1.0 - lam_init)).astype(o_ref.dtype)


def diff_attn_prompt(q, k, v, slopes, lparams, subln, *, batch, seq, n_heads, dh, dv, lam_init):
    t = _pick(seq, (512, 256, 128))
    nq = seq // t
    group = _pick(n_heads, (4, 2, 1))
    vec = lambda n: pl.BlockSpec((1, n), lambda b, h, i: (0, 0))
    return pl.pallas_call(
        functools.partial(_diff_attn_kernel, t=t, dh=dh, dv=dv, group=group, lam_init=lam_init),
        grid=(batch, n_heads // group, nq),
        in_specs=[pl.BlockSpec(memory_space=pltpu.SMEM),
                  pl.BlockSpec((t, group * 2 * dh), lambda b, h, i: (b * nq + i, h)),
                  pl.BlockSpec((seq, group * 2 * dh), lambda b, h, i: (b, h)),
                  pl.BlockSpec((seq, group * dv), lambda b, h, i: (b, h)),
                  vec(dh), vec(dh), vec(dh), vec(dh), vec(dv)],
        out_specs=pl.BlockSpec((t, group * dv), lambda b, h, i: (b * nq + i, h)),
        out_shape=jax.ShapeDtypeStruct((batch * seq, n_heads * dv), BF16),
        scratch_shapes=[pltpu.VMEM((group, 2 * t, seq), F32),
                        pltpu.VMEM((group, 2 * t, V7X_LANES), F32),
                        pltpu.VMEM((group, 2 * t, V7X_LANES), F32),
                        pltpu.VMEM((group, 2 * t, dv), F32)],
        compiler_params=_params("parallel", "parallel", "arbitrary"),
        name="diff_attn_prompt",
    )(slopes, q, k, v, *lparams, subln)


def _retention_kernel(q_ref, k_ref, v_ref, g_ref, norm_ref, y_ref, s_ref, *, c, n_heads, dk, dv,
                      log_gammas):
    step = pl.program_id(1)

    @pl.when(step == 0)
    def _():
        s_ref[...] = jnp.zeros_like(s_ref)

    row = lax.broadcasted_iota(jnp.int32, (c, c), 0)
    col = lax.broadcasted_iota(jnp.int32, (c, c), 1)
    diff = (row - col).astype(F32)
    n_v = lax.broadcasted_iota(jnp.int32, (c, dv), 0).astype(F32)
    n_k = lax.broadcasted_iota(jnp.int32, (c, dk), 0).astype(F32)
    scale = dk ** -0.5
    for h in range(n_heads):
        lg = log_gammas[h]
        q = q_ref[:, h * dk:(h + 1) * dk]
        k = k_ref[:, h * dk:(h + 1) * dk]
        v = v_ref[:, h * dv:(h + 1) * dv]
        state = s_ref[h]
        dmask = jnp.where(diff >= 0, jnp.exp(lg * jnp.maximum(diff, 0.0)), 0.0) * scale
        inner = _dot_nt(q, k) * dmask
        o = _dot(inner.astype(BF16), v)
        o = o + _dot(q, state.astype(BF16)) * jnp.exp(lg * (n_v + 1.0))
        k_dec = (k.astype(F32) * (jnp.exp(lg * (c - 1.0 - n_k)) * scale)).T
        s_ref[h] = math.exp(lg * c) * state + _dot(k_dec.astype(BF16), v)
        gate = g_ref[:, h * dv:(h + 1) * dv]
        y = _rms(o, norm_ref[...]) * (gate * jax.nn.sigmoid(gate))
        y_ref[:, h * dv:(h + 1) * dv] = y.astype(y_ref.dtype)


def retention_prompt(pa, g_r, ret_norm, *, batch, seq, n_heads, dk, dv, chunk, log_gammas,
                     q_col, k_col, v_col):
    nc = seq // chunk
    wq = n_heads * dk
    wv = n_heads * dv
    return pl.pallas_call(
        functools.partial(_retention_kernel, c=chunk, n_heads=n_heads, dk=dk, dv=dv,
                          log_gammas=log_gammas),
        grid=(batch, nc),
        in_specs=[pl.BlockSpec((chunk, wq), lambda b, s: (b * nc + s, q_col // wq)),
                  pl.BlockSpec((chunk, wq), lambda b, s: (b * nc + s, k_col // wq)),
                  pl.BlockSpec((chunk, wv), lambda b, s: (b * nc + s, v_col // wv)),
                  pl.BlockSpec((chunk, wv), lambda b, s: (b * nc + s, 0)),
                  pl.BlockSpec((1, dv), lambda b, s: (0, 0))],
        out_specs=(pl.BlockSpec((chunk, wv), lambda b, s: (b * nc + s, 0)),
                   pl.BlockSpec((None, n_heads, dk, dv), lambda b, s: (b, 0, 0, 0))),
        out_shape=(jax.ShapeDtypeStruct((batch * seq, wv), BF16),
                   jax.ShapeDtypeStruct((batch, n_heads, dk, dv), F32)),
        compiler_params=_params("parallel", "arbitrary"),
        name="retention_prompt",
    )(pa, pa, pa, g_r, ret_norm.reshape(1, dv))


def _cross_block_kernel(x_ref, g_ref, wq_ref, k_ref, v_ref, wo_ref, o_ref, *, n_heads, dh):
    x = x_ref[...]
    q = _dot(_rms(x, g_ref[...]).astype(BF16), wq_ref[...].astype(BF16)).astype(BF16)
    heads = []
    for h in range(n_heads):
        sl = slice(h * dh, (h + 1) * dh)
        s = _dot_nt(q[:, sl], k_ref[:, sl]) * (dh ** -0.5)
        m = jnp.max(s, axis=-1, keepdims=True)
        p = jnp.exp(s - m)
        p = p / jnp.sum(p, axis=-1, keepdims=True)
        heads.append(_dot(p.astype(BF16), v_ref[:, sl]).astype(BF16))
    o_ref[...] = x + _dot(jnp.concatenate(heads, axis=1), wo_ref[...].astype(BF16))


def cross_block_prompt(x, g, wq, mkv, wo, layer, *, batch, seq, n_mem, n_heads, dh):
    d = x.shape[1]
    w = n_heads * dh
    tq = _pick(seq, (512, 256, 128))
    nq = seq // tq
    row_blk = pl.BlockSpec((tq, d), lambda b, i: (b * nq + i, 0))
    return pl.pallas_call(
        functools.partial(_cross_block_kernel, n_heads=n_heads, dh=dh),
        grid=(batch, nq),
        in_specs=[row_blk,
                  pl.BlockSpec((1, d), lambda b, i: (0, 0)),
                  pl.BlockSpec((None, d, w), lambda b, i: (layer, 0, 0)),
                  pl.BlockSpec((n_mem, w), lambda b, i: (b, 0)),
                  pl.BlockSpec((n_mem, w), lambda b, i: (b, 1)),
                  pl.BlockSpec((None, w, d), lambda b, i: (layer, 0, 0))],
        out_specs=row_blk,
        out_shape=jax.ShapeDtypeStruct(x.shape, F32),
        compiler_params=_params("parallel", "parallel"),
        name="cross_block_prompt",
    )(x, g.reshape(1, d), wq, mkv, mkv, wo)


def _diff_decode_kernel(*refs, n_pages, page, n_heads, dh, lam_init):
    pos = 1
    qt_ref, knt_ref, vn_ref = refs[pos:pos + 3]; pos += 3
    kt_refs = refs[pos:pos + n_pages]; pos += n_pages
    v_refs = refs[pos:pos + n_pages]; pos += n_pages
    slopes_ref = refs[pos]; pos += 1
    lparams = refs[pos:pos + 4]; pos += 4
    subln_ref = refs[pos]; pos += 1
    o_ref = refs[pos]

    b = pl.program_id(0)
    nb = qt_ref.shape[1]
    onehot = (lax.broadcasted_iota(jnp.int32, (nb, V7X_LANES), 0) == b).astype(BF16)
    q_rep = _dot(qt_ref[...], onehot) * (dh ** -0.5)
    kn_rep = _dot(knt_ref[...], onehot)

    def map_sums(x):
        x3 = x.reshape(n_heads, 2 * dh, x.shape[-1])
        return x3[:, :dh, :].sum(axis=1), x3[:, dh:, :].sum(axis=1)

    past = n_pages * page
    tpos = lax.broadcasted_iota(jnp.int32, (n_heads, page), 1).astype(F32)
    nslope = -slopes_ref[...]
    s1, s2 = [], []
    for p in range(n_pages):
        a, c = map_sums(kt_refs[p][...] * q_rep)
        bias = nslope * ((past - p * page) - tpos)
        s1.append(a + bias)
        s2.append(c + bias)
    n1, n2 = map_sums(kn_rep * q_rep)

    def softmax_parts(s_list, s_new):
        m = s_list[0]
        for s in s_list[1:]:
            m = jnp.maximum(m, s)
        m = jnp.maximum(jnp.max(m, axis=-1, keepdims=True), s_new)
        e = [jnp.exp(s - m) for s in s_list]
        e_new = jnp.exp(s_new - m)
        l = e[0]
        for x in e[1:]:
            l = l + x
        l = jnp.sum(l, axis=-1, keepdims=True) + e_new
        return e, e_new, 1.0 / l

    e1, en1, a1 = softmax_parts(s1, n1)
    e2, en2, a2 = softmax_parts(s2, n2)
    a2 = a2 * _diff_lambda(*lparams, lam_init)

    tok = lax.broadcasted_iota(jnp.int32, (page, n_heads, V7X_LANES), 0)
    lane = lax.broadcasted_iota(jnp.int32, (page, n_heads, V7X_LANES), 2)
    diag = tok == lane
    ones = jnp.ones((V7X_LANES, V7X_LANES), BF16)
    o = (en1 * a1 - en2 * a2) * vn_ref[...]
    for p in range(n_pages):
        w = e1[p] * a1 - e2[p] * a2
        wd = jnp.where(diag, w[None], 0.0).reshape(page * n_heads, V7X_LANES).astype(BF16)
        w3 = _dot(wd, ones).reshape(page, n_heads, V7X_LANES)
        o = o + jnp.sum(w3 * v_refs[p][...], axis=0)
    o_ref[...] = (_rms(o, subln_ref[...]) * (1.0 - lam_init)).astype(o_ref.dtype)


def diff_attn_sample(q, k_new, v_new, cache_kt, cache_v, layer, page_table, slopes, lparams, subln, *,
                     n_heads, dh, dv, lam_init):
    nb, n_pages = page_table.shape
    page = cache_v.shape[2]
    wk = n_heads * 2 * dh
    assert page == V7X_LANES and dv == V7X_LANES and n_heads == V7X_SUBLANES
    const = lambda shape: pl.BlockSpec(shape, lambda b, pt: (0,) * len(shape))

    def kt_spec(p):
        return pl.BlockSpec((None, None, wk, page), lambda b, pt: (layer, pt[b, p], 0, 0))

    def v_spec(p):
        return pl.BlockSpec((None, None, page, n_heads, dv), lambda b, pt: (layer, pt[b, p], 0, 0, 0))

    in_specs = ([const((wk, nb)), const((wk, nb)),
                 pl.BlockSpec((None, n_heads, dv), lambda b, pt: (b, 0, 0))]
                + [kt_spec(p) for p in range(n_pages)] + [v_spec(p) for p in range(n_pages)]
                + [const((n_heads, V7X_LANES))] + [const((1, dh))] * 4 + [const((1, dv))])
    out = pl.pallas_call(
        functools.partial(_diff_decode_kernel, n_pages=n_pages, page=page, n_heads=n_heads, dh=dh,
                          lam_init=lam_init),
        grid_spec=pltpu.PrefetchScalarGridSpec(
            num_scalar_prefetch=1, grid=(nb,), in_specs=in_specs,
            out_specs=pl.BlockSpec((None, n_heads, dv), lambda b, pt: (b, 0, 0))),
        out_shape=jax.ShapeDtypeStruct((nb, n_heads, dv), BF16),
        compiler_params=_params("arbitrary"),
        name="diff_attn_sample",
    )(page_table, q.T.astype(BF16), k_new.T.astype(BF16), v_new.reshape(nb, n_heads, dv),
      *([cache_kt] * n_pages), *([cache_v] * n_pages),
      jnp.broadcast_to(slopes[:, None], (n_heads, V7X_LANES)), *lparams, subln)
    return out.reshape(nb, n_heads * dv)


def _cross_decode_kernel(q_ref, k_ref, v_ref, o_ref, *, n_heads, dh):
    n_req, n_tiles = k_ref.shape[:2]
    ones = jnp.ones((dh, V7X_LANES), BF16)

    def over_tokens(x, op):
        out = x
        for r in range(n_heads, V7X_SUBLANES, n_heads):
            out = op(out, pltpu.roll(x, r, axis=0))
        return out

    for r in range(n_req):
        q = q_ref[r] * (dh ** -0.5)
        prod = (k_ref[r] * q[None]).reshape(n_tiles * V7X_SUBLANES, dh).astype(BF16)
        s = _dot(prod, ones).reshape(n_tiles, V7X_SUBLANES, V7X_LANES)
        m = over_tokens(jnp.max(s, axis=0), jnp.maximum)
        e = jnp.exp(s - m[None])
        l = over_tokens(jnp.sum(e, axis=0), jnp.add)
        o = over_tokens(jnp.sum(e * v_ref[r], axis=0), jnp.add)
        o_ref[r] = o / l


def cross_attn_sample(q, cache_k, cache_v, layer, *, n_heads, dh):
    nb = q.shape[0]
    assert dh == V7X_LANES and V7X_SUBLANES % n_heads == 0
    n_tiles = cache_k.shape[2]
    q8 = jnp.tile(q.reshape(nb, n_heads, dh), (1, V7X_SUBLANES // n_heads, 1))
    rb = _pick(nb, (4, 2, 1))
    kv_spec = pl.BlockSpec((None, rb, n_tiles, V7X_SUBLANES, dh), lambda b: (layer, b, 0, 0, 0))
    row = pl.BlockSpec((rb, V7X_SUBLANES, dh), lambda b: (b, 0, 0))
    out = pl.pallas_call(
        functools.partial(_cross_decode_kernel, n_heads=n_heads, dh=dh),
        grid=(nb // rb,),
        in_specs=[row, kv_spec, kv_spec],
        out_specs=row,
        out_shape=jax.ShapeDtypeStruct((nb, V7X_SUBLANES, dh), F32),
        compiler_params=_params("parallel"),
        name="cross_attn_sample",
    )(q8, cache_k, cache_v)
    return out[:, :n_heads].reshape(nb, n_heads * dh)


def _retention_step_kernel(q_ref, k_ref, v_ref, s_ref, *rest, n_heads, dk, dv, log_gammas):
    o_ref, sn_ref = rest[-2:]
    scale = dk ** -0.5
    eye = (lax.broadcasted_iota(jnp.int32, (dk, dk), 0)
           == lax.broadcasted_iota(jnp.int32, (dk, dk), 1))
    for r in range(q_ref.shape[0]):
        for h in range(n_heads):
            gamma = math.exp(log_gammas[h])
            q = q_ref[r, :, h * dk:(h + 1) * dk]
            k = k_ref[r, :, h * dk:(h + 1) * dk]
            v = v_ref[r, :, h * dv:(h + 1) * dv]
            state = s_ref[r, h]
            inner = jnp.sum(q * k, axis=-1, keepdims=True) * scale
            q8 = jnp.broadcast_to(q, (V7X_SUBLANES, dk)).astype(BF16)
            o_cross = _dot(q8, state.astype(BF16))[0:1] * gamma
            o_ref[r, :, h * dv:(h + 1) * dv] = inner * v + o_cross
            k_diag = jnp.where(eye, jnp.broadcast_to(k * scale, (dk, dk)), 0.0).astype(BF16)
            v_rows = jnp.broadcast_to(v, (dk, dv)).astype(BF16)
            sn_ref[r, h] = gamma * state + _dot(k_diag, v_rows)


def retention_sample(q, k, v, state, layer, stacked, *, n_heads, dk, dv, log_gammas):
    nb = q.shape[0]
    wq, wv = n_heads * dk, n_heads * dv
    rb = _pick(nb, (4, 2, 1))
    row = lambda w: pl.BlockSpec((rb, 1, w), lambda b, *_: (b, 0, 0))
    state_blk = pl.BlockSpec((None, rb, n_heads, dk, dv), lambda b, *_: (layer, b, 0, 0, 0))
    ins = [q.reshape(nb, 1, wq), k.reshape(nb, 1, wq), v.reshape(nb, 1, wv), state]
    in_specs = [row(wq), row(wq), row(wv), state_blk]
    aliases = _stacked(in_specs, ins, stacked, n_out_before=1)
    o, s_new = pl.pallas_call(
        functools.partial(_retention_step_kernel, n_heads=n_heads, dk=dk, dv=dv,
                          log_gammas=log_gammas),
        grid=(nb // rb,),
        in_specs=in_specs,
        out_specs=(row(wv), state_blk),
        out_shape=(jax.ShapeDtypeStruct((nb, 1, wv), F32),
                   jax.ShapeDtypeStruct(state.shape, F32)),
        input_output_aliases=aliases,
        compiler_params=_params("parallel"),
        name="retention_sample",
    )(*ins)
    return o.reshape(nb, wv), s_new


def kernel(x_prompt, x_sample, mem_prompt, cache_k_diff, cache_v_diff, cache_mem_k, cache_mem_v,
           state_ret, page_table, norm_mix, w_in, w_out, da_lq1, da_lk1, da_lq2, da_lk2, da_subln,
           ret_norm, norm_x, norm_mem, wx_q, wx_kv, wx_o, norm_ff, w_up, w_down, norm_final):
    batch, seq, d_model = x_prompt.shape
    dec_batch, dec_seq, _ = x_sample.shape
    assert dec_seq == 1
    depth, n_pool, page, h_da, _, dh_da = cache_k_diff.shape
    dv_da = cache_v_diff.shape[-1]
    _, _, h_ret, dk_ret, dv_ret = state_ret.shape
    _, _, n_mem, h_x, dh_x = cache_mem_k.shape
    d_qk = h_da * 2 * dh_da
    d_da = h_da * dv_da
    d_rqk = h_ret * dk_ret
    d_ret = h_ret * dv_ret
    d_x = h_x * dh_x
    chunk = _pick(seq, (256, 128))
    mp = batch * seq

    slopes = 2.0 ** (-8.0 * (jnp.arange(h_da, dtype=F32) + 1.0) / h_da)
    log_gammas = tuple(math.log(1.0 - 2.0 ** (-5.0 - h)) for h in range(h_ret))

    o_q, o_k, o_v = 0, d_qk, 2 * d_qk
    o_qr = o_v + d_da
    o_kr, o_vr = o_qr + d_rqk, o_qr + 2 * d_rqk
    o_g = o_vr + d_ret

    cache_kt = jnp.transpose(cache_k_diff, (0, 1, 3, 4, 5, 2)).reshape(depth, n_pool, d_qk, page)
    mem_tiles = n_mem * h_x // V7X_SUBLANES
    mem_k = cache_mem_k.reshape(depth, dec_batch, mem_tiles, V7X_SUBLANES, dh_x)
    mem_v = cache_mem_v.reshape(depth, dec_batch, mem_tiles, V7X_SUBLANES, dh_x)

    xp = x_prompt.reshape(mp, d_model)
    xs = x_sample.reshape(dec_batch, d_model)
    mem = mem_prompt.reshape(batch * n_mem, d_model)

    wu_all, wd_all = w_up.astype(BF16), w_down.astype(BF16)
    outs = {name: [] for name in ("ps", "pmk", "pmv", "sk", "sv")}
    pk_all = pv_all = ss_all = None
    for l in range(depth):
        lam_init = 0.8 - 0.6 * math.exp(-0.3 * l)
        lparams = tuple(a[l].reshape(1, dh_da) for a in (da_lq1, da_lk1, da_lq2, da_lk2))
        subln = da_subln[l].reshape(1, dv_da)
        seg = lambda c0, c1: ((c0, c1 - c0),)
        w_a = WView(w_in, l, 0, d_model, seg(o_q, o_k) + seg(o_qr, o_g))
        w_k = WView(w_in, l, 0, d_model, seg(o_k, o_v))
        w_v = WView(w_in, l, 0, d_model, seg(o_v, o_qr))
        w_g = WView(w_in, l, 0, d_model, seg(o_g, o_g + d_ret))
        w_out_da = WView(w_out, l, 0, d_da, seg(0, d_model))
        w_out_r = WView(w_out, l, d_da, d_ret, seg(0, d_model))
        wq_b = WView(wx_q, l, 0, d_model, seg(0, d_x))
        wkv_b = WView(wx_kv, l, 0, d_model, seg(0, 2 * d_x))
        wo_b = WView(wx_o, l, 0, d_x, seg(0, d_model))
        a_q, a_qr, a_kr, a_vr = 0, d_qk, d_qk + d_rqk, d_qk + 2 * d_rqk

        last = l == depth - 1
        g_next = norm_final if last else norm_mix[l + 1]

        h = rmsnorm(xp, norm_mix[l], BF16) if l == 0 else hp_next
        (pa,) = matmul([(h, w_a)], (BF16,))
        pk_all, k_b = kproj(h, w_k, l, depth, pk_all, batch=batch, seq=seq)
        pv_all, v_b = vproj(h, w_v, l, depth, pv_all, n_heads=h_da)
        (g_r,) = matmul([(h, w_g)], (F32,))
        y_da = diff_attn_prompt(pa, k_b, v_b, slopes, lparams, subln,
                                batch=batch, seq=seq, n_heads=h_da, dh=dh_da, dv=dv_da, lam_init=lam_init)
        y_r, s_p = retention_prompt(pa, g_r, ret_norm[l], batch=batch, seq=seq, n_heads=h_ret,
                                    dk=dk_ret, dv=dv_ret, chunk=chunk, log_gammas=log_gammas,
                                    q_col=a_qr, k_col=a_kr, v_col=a_vr)
        (xp,) = matmul([(y_da, w_out_da), (y_r, w_out_r)], (F32,), residual=xp)
        hm = rmsnorm(mem, norm_mem[l], BF16)
        mkv_f32, mkv_b = matmul([(hm, wkv_b)], (F32, BF16))
        xp = cross_block_prompt(xp, norm_x[l], wx_q, mkv_b, wx_o, l, batch=batch, seq=seq, n_mem=n_mem,
                                n_heads=h_x, dh=dh_x)
        if last:
            (y_prompt,) = mlp(xp, norm_ff[l], wu_all, wd_all, l, g_next, keep_x=False)
        else:
            xp, hp_next = mlp(xp, norm_ff[l], wu_all, wd_all, l, g_next, keep_x=True)
        outs["ps"].append(s_p)
        outs["pmk"].append(mkv_f32[:, :d_x].reshape(batch, n_mem, h_x, dh_x))
        outs["pmv"].append(mkv_f32[:, d_x:].reshape(batch, n_mem, h_x, dh_x))

        h = rmsnorm(xs, norm_mix[l], BF16) if l == 0 else hs_next
        (ps_a,) = matmul([(h, w_a)], (F32,))
        (k_new,) = matmul([(h, w_k)], (F32,))
        (v_new,) = matmul([(h, w_v)], (F32,))
        (g_s,) = matmul([(h, w_g)], (F32,))
        y_da = diff_attn_sample(ps_a[:, a_q:a_qr], k_new, v_new, cache_kt, cache_v_diff, l, page_table,
                                slopes, lparams, subln, n_heads=h_da, dh=dh_da, dv=dv_da,
                                lam_init=lam_init)
        o_r, ss_all = retention_sample(ps_a[:, a_qr:a_kr], ps_a[:, a_kr:a_vr], ps_a[:, a_vr:], state_ret, l, ss_all,
                                    n_heads=h_ret, dk=dk_ret, dv=dv_ret, log_gammas=log_gammas)
        y_r = gated_headnorm(o_r.reshape(dec_batch * h_ret, dv_ret), ret_norm[l],
                             g_s.reshape(dec_batch * h_ret, dv_ret))
        (xs,) = matmul([(y_da, w_out_da), (y_r.reshape(dec_batch, d_ret), w_out_r)],
                       (F32,), residual=xs)
        hx = rmsnorm(xs, norm_x[l], BF16)
        (qx,) = matmul([(hx, wq_b)], (F32,))
        ox = cross_attn_sample(qx, mem_k, mem_v, l, n_heads=h_x, dh=dh_x)
        (xs,) = matmul([(ox.astype(BF16), wo_b)], (F32,), residual=xs)
        if last:
            (y_sample,) = mlp(xs, norm_ff[l], wu_all, wd_all, l, g_next, keep_x=False)
        else:
            xs, hs_next = mlp(xs, norm_ff[l], wu_all, wd_all, l, g_next, keep_x=True)
        outs["sk"].append(k_new.reshape(dec_batch, 1, h_da, 2, dh_da))
        outs["sv"].append(v_new.reshape(dec_batch, 1, h_da, dv_da))

    y_prompt = y_prompt.reshape(batch, seq, d_model)
    y_sample = y_sample.reshape(dec_batch, 1, d_model)
    st = lambda name: jnp.stack(outs[name])
    new_k_prompt = jnp.transpose(pk_all.reshape(depth, batch, h_da, 2, dh_da, seq), (0, 1, 5, 2, 3, 4))
    return (y_prompt, y_sample, new_k_prompt, pv_all.reshape(depth, batch, seq, h_da, dv_da), st("ps"),
            st("pmk"), st("pmv"), st("sk"), st("sv"), ss_all)
```

```python
import functools
import math
from typing import NamedTuple

import jax
import jax.numpy as jnp
from jax import lax
from jax.experimental import pallas as pl
from jax.experimental.pallas import tpu as pltpu

EPS = 1e-5
NEG_INF = -1e30
LOG2E = math.log2(math.e)
F32 = jnp.float32
BF16 = jnp.bfloat16

V7X_VMEM_LIMIT_BYTES = 60 * 1024 * 1024
V7X_LANES = 128
V7X_SUBLANES = 8


def _params(*semantics):
    return pltpu.CompilerParams(dimension_semantics=semantics,
                                vmem_limit_bytes=V7X_VMEM_LIMIT_BYTES)


def _pick(n, candidates):
    for c in candidates:
        if n % c == 0:
            return c
    return n


def _dot(a, b):
    return jnp.dot(a, b, preferred_element_type=F32)


def _dot_nt(a, b):
    return lax.dot_general(a, b, (((1,), (1,)), ((), ())), preferred_element_type=F32)


class WView(NamedTuple):
    arr: jax.Array
    layer: int
    row0: int
    k: int
    segs: tuple

    @property
    def n(self):
        return sum(width for _, width in self.segs)

    def spec(self, tn, grid_rank):
        assert self.row0 % self.k == 0 and all(c % tn == 0 and w % tn == 0 for c, w in self.segs)
        blocks = [c // tn + t for c, w in self.segs for t in range(w // tn)]
        layer, row_blk = self.layer, self.row0 // self.k

        def col(j):
            blk = blocks[-1]
            for idx in range(len(blocks) - 2, -1, -1):
                blk = jnp.where(j == idx, blocks[idx], blk)
            return blk

        if len(blocks) == 1:
            return pl.BlockSpec((None, self.k, tn), lambda *g: (layer, row_blk, blocks[0]))
        assert grid_rank == 2
        return pl.BlockSpec((None, self.k, tn), lambda j, i: (layer, row_blk, col(j)))


def _rms(x, g):
    ms = jnp.mean(x * x, axis=-1, keepdims=True)
    return x * lax.rsqrt(ms + EPS) * g


def _rmsnorm_kernel(x_ref, g_ref, o_ref):
    o_ref[...] = _rms(x_ref[...], g_ref[...]).astype(o_ref.dtype)


def rmsnorm(x, g, out_dtype):
    m, d = x.shape
    tm = _pick(m, (512, 256, 128))
    return pl.pallas_call(
        _rmsnorm_kernel,
        grid=(m // tm,),
        in_specs=[pl.BlockSpec((tm, d), lambda i: (i, 0)),
                  pl.BlockSpec((1, d), lambda i: (0, 0))],
        out_specs=pl.BlockSpec((tm, d), lambda i: (i, 0)),
        out_shape=jax.ShapeDtypeStruct((m, d), out_dtype),
        compiler_params=_params("parallel"),
        name="rmsnorm",
    )(x, g.reshape(1, d))


def _gated_headnorm_kernel(x_ref, g_ref, gate_ref, o_ref):
    gate = gate_ref[...]
    y = _rms(x_ref[...], g_ref[...]) * (gate * jax.nn.sigmoid(gate))
    o_ref[...] = y.astype(o_ref.dtype)


def gated_headnorm(x, g, gate):
    r, d = x.shape
    tr = _pick(r, (512, 256, 128))
    blk = pl.BlockSpec((tr, d), lambda i: (i, 0))
    return pl.pallas_call(
        _gated_headnorm_kernel,
        grid=(r // tr,),
        in_specs=[blk, pl.BlockSpec((1, d), lambda i: (0, 0)), blk],
        out_specs=blk,
        out_shape=jax.ShapeDtypeStruct((r, d), BF16),
        compiler_params=_params("parallel"),
        name="gated_headnorm",
    )(x, g.reshape(1, d), gate)


def _mm_kernel(*refs, n_pairs, has_res):
    acc = _dot(refs[0][...], refs[1][...].astype(BF16))
    for p in range(1, n_pairs):
        acc = acc + _dot(refs[2 * p][...], refs[2 * p + 1][...].astype(BF16))
    pos = 2 * n_pairs
    if has_res:
        acc = refs[pos][...] + acc
        pos += 1
    for o_ref in refs[pos:]:
        o_ref[...] = acc.astype(o_ref.dtype)


def matmul(pairs, out_dtypes, residual=None):
    m = pairs[0][0].shape[0]
    n = pairs[0][1].n
    tm = _pick(m, (1024, 512, 256, 128))
    tn = _pick(n, (1024, 512, 256, 128))
    ins, specs = [], []
    for a, w in pairs:
        assert a.shape[1] == w.k and w.n == n
        ins += [a, w.arr]
        specs += [pl.BlockSpec((tm, w.k), lambda j, i: (i, 0)), w.spec(tn, 2)]
    if residual is not None:
        ins.append(residual)
        specs.append(pl.BlockSpec((tm, tn), lambda j, i: (i, j)))
    return pl.pallas_call(
        functools.partial(_mm_kernel, n_pairs=len(pairs), has_res=residual is not None),
        grid=(n // tn, m // tm),
        in_specs=specs,
        out_specs=tuple(pl.BlockSpec((tm, tn), lambda j, i: (i, j)) for _ in out_dtypes),
        out_shape=tuple(jax.ShapeDtypeStruct((m, n), dt) for dt in out_dtypes),
        compiler_params=_params("parallel", "parallel"),
        name="matmul",
    )(*ins)


def _vproj_kernel(a_ref, w_ref, *rest, n_heads):
    o4_ref, ob_ref = rest[-2:]
    acc = _dot(a_ref[...], w_ref[...].astype(BF16))
    ob_ref[...] = acc.astype(BF16)
    dv = acc.shape[1] // n_heads
    for h in range(n_heads):
        o4_ref[:, h, :] = acc[:, h * dv:(h + 1) * dv]


def vproj(a, w, layer, depth, stacked, *, n_heads):
    m, k = a.shape
    n = w.n
    dv = n // n_heads
    tm = _pick(m, (512, 256, 128))
    ins = [a, w.arr]
    in_specs = [pl.BlockSpec((tm, k), lambda i: (i, 0)), w.spec(n, 1)]
    aliases = _stacked(in_specs, ins, stacked)
    return pl.pallas_call(
        functools.partial(_vproj_kernel, n_heads=n_heads),
        grid=(m // tm,),
        in_specs=in_specs,
        out_specs=(pl.BlockSpec((None, tm, n_heads, dv), lambda i: (layer, i, 0, 0)),
                   pl.BlockSpec((tm, n), lambda i: (i, 0))),
        out_shape=(jax.ShapeDtypeStruct((depth, m, n_heads, dv), F32),
                   jax.ShapeDtypeStruct((m, n), BF16)),
        input_output_aliases=aliases,
        compiler_params=_params("parallel"),
        name="vproj",
    )(*ins)


def _kproj_kernel(*refs, norm):
    if norm:
        x_ref, g_ref, w_ref = refs[:3]
        ot_ref, ob_ref, hn_ref = refs[-3:]
        h = _rms(x_ref[...], g_ref[...]).astype(BF16)
        hn_ref[...] = h
    else:
        h_ref, w_ref = refs[:2]
        ot_ref, ob_ref = refs[-2:]
        h = h_ref[...]
    k = _dot(h, w_ref[...].astype(BF16))
    ob_ref[...] = k.astype(BF16)
    ot_ref[...] = k.T


def kproj(h, w, layer, depth, stacked, *, batch, seq, norm_gain=None):
    d = h.shape[1]
    n = w.n
    ts = _pick(seq, (1024, 512, 256, 128))
    ns = seq // ts
    norm = norm_gain is not None
    assert not (norm and stacked is not None)
    row_blk = lambda width: pl.BlockSpec((ts, width), lambda b, i: (b * ns + i, 0))
    ins = [h] + ([norm_gain.reshape(1, d)] if norm else []) + [w.arr]
    in_specs = [row_blk(d)] + ([pl.BlockSpec((1, d), lambda b, i: (0, 0))] if norm else []) + [w.spec(n, 1)]
    aliases = _stacked(in_specs, ins, stacked)
    out_specs = [pl.BlockSpec((None, None, n, ts), lambda b, i: (layer, b, 0, i)), row_blk(n)]
    out_shape = [jax.ShapeDtypeStruct((depth, batch, n, seq), F32),
                 jax.ShapeDtypeStruct((batch * seq, n), BF16)]
    if norm:
        out_specs.append(row_blk(d))
        out_shape.append(jax.ShapeDtypeStruct((batch * seq, d), BF16))
    return pl.pallas_call(
        functools.partial(_kproj_kernel, norm=norm),
        grid=(batch, ns),
        in_specs=in_specs,
        out_specs=tuple(out_specs),
        out_shape=tuple(out_shape),
        input_output_aliases=aliases,
        compiler_params=_params("parallel", "parallel"),
        name="kproj",
    )(*ins)


def _stacked(spec_list, ins, stacked, n_out_before=0):
    if stacked is None:
        return {}
    ins.append(stacked)
    spec_list.append(pl.BlockSpec(memory_space=pl.ANY))
    return {len(ins) - 1: n_out_before}


def _mlp_kernel(x_ref, g_ref, wu_ref, wd_ref, gn_ref, *refs, keep_x):
    acc_ref, h_ref = refs[0], refs[-1]
    f = pl.program_id(1)

    @pl.when(f == 0)
    def _():
        h_ref[...] = _rms(x_ref[...], g_ref[...]).astype(BF16)

    u = _dot(h_ref[...], wu_ref[...])
    a = jnp.square(jnp.maximum(u, 0.0)).astype(BF16)
    c = _dot(a, wd_ref[...])

    @pl.when(f == 0)
    def _():
        acc_ref[...] = x_ref[...] + c

    @pl.when(f != 0)
    def _():
        acc_ref[...] += c

    @pl.when(f == pl.num_programs(1) - 1)
    def _():
        n_ref = refs[1] if keep_x else acc_ref
        n_ref[...] = _rms(acc_ref[...], gn_ref[...]).astype(n_ref.dtype)


def mlp(x, g, w_up, w_down, layer, g_next, *, keep_x):
    m, d = x.shape
    ff = w_up.shape[2]
    tm = _pick(m, (512, 256, 128))
    tf = _pick(ff, (2048, 1024, 512, 256, 128))
    row_blk = pl.BlockSpec((tm, d), lambda i, f: (i, 0))
    vec = pl.BlockSpec((1, d), lambda i, f: (0, 0))
    out_shape = [jax.ShapeDtypeStruct((m, d), F32)]
    if keep_x:
        out_shape.append(jax.ShapeDtypeStruct((m, d), BF16))
    return pl.pallas_call(
        functools.partial(_mlp_kernel, keep_x=keep_x),
        grid=(m // tm, ff // tf),
        in_specs=[row_blk, vec,
                  pl.BlockSpec((None, d, tf), lambda i, f: (layer, 0, f)),
                  pl.BlockSpec((None, tf, d), lambda i, f: (layer, f, 0)),
                  vec],
        out_specs=tuple(row_blk for _ in out_shape),
        out_shape=tuple(out_shape),
        scratch_shapes=[pltpu.VMEM((tm, d), BF16)],
        compiler_params=_params("parallel", "arbitrary"),
        name="mlp",
    )(x, g.reshape(1, d), w_up, w_down, g_next.reshape(1, d))


def _diff_lambda(lq1_ref, lk1_ref, lq2_ref, lk2_ref, lam_init):
    e1 = jnp.exp(jnp.sum(lq1_ref[...] * lk1_ref[...], axis=-1, keepdims=True))
    e2 = jnp.exp(jnp.sum(lq2_ref[...] * lk2_ref[...], axis=-1, keepdims=True))
    return e1 - e2 + lam_init


def _lane_fold(x, op):
    out = x[:, :V7X_LANES]
    for c in range(V7X_LANES, x.shape[1], V7X_LANES):
        out = op(out, x[:, c:c + V7X_LANES])
    return out


def _diff_attn_kernel(slopes_ref, q_ref, k_ref, v_ref, lq1_ref, lk1_ref, lq2_ref, lk2_ref,
                      subln_ref, o_ref, s_ref, m_ref, l_ref, acc_ref, *, t, dh, dv, group, lam_init):
    hg = pl.program_id(1)
    i = pl.program_id(2)
    kpos0 = lax.broadcasted_iota(jnp.int32, (1, t), 1).astype(F32)
    lane = lax.broadcasted_iota(jnp.int32, (t, 2 * dh), 1)

    qq, slope2 = [], []
    for g in range(group):
        q = q_ref[:, g * 2 * dh:(g + 1) * 2 * dh].astype(F32) * (dh ** -0.5 * LOG2E)
        qq.append(jnp.concatenate([jnp.where(lane < dh, q, 0.0), jnp.where(lane >= dh, q, 0.0)],
                                  axis=0).astype(BF16))
        slope2.append(slopes_ref[hg * group + g] * LOG2E)

    def scores(g, j):
        start = pl.multiple_of(j * t, t)
        kpos = kpos0 + (j * t).astype(F32)
        return _dot_nt(qq[g], k_ref[pl.ds(start, t), g * 2 * dh:(g + 1) * 2 * dh]) + slope2[g] * kpos

    def pass1(j, carry):
        for g in range(group):
            s = scores(g, j)
            s_ref[g, :, pl.ds(pl.multiple_of(j * t, t), t)] = s
            m_ref[g] = jnp.maximum(m_ref[g], _lane_fold(s, jnp.maximum))
        return carry

    row = lax.broadcasted_iota(jnp.int32, (2 * t, t), 0)
    col = lax.broadcasted_iota(jnp.int32, (2 * t, t), 1)
    visible = jnp.where(row >= t, row - t, row) >= col
    for g in range(group):
        s = jnp.where(visible, scores(g, i), NEG_INF)
        s_ref[g, :, pl.ds(pl.multiple_of(i * t, t), t)] = s
        m_ref[g] = _lane_fold(s, jnp.maximum)
    lax.fori_loop(0, i, pass1, 0)

    def pass2_block(g, j, mb, first):
        start = pl.multiple_of(j * t, t)
        ps = []
        lsum = None if first else l_ref[g]
        for c in range(0, t, V7X_LANES):
            p = jnp.exp2(s_ref[g, :, pl.ds(start + c, V7X_LANES)] - mb)
            lsum = p if lsum is None else lsum + p
            ps.append(p.astype(BF16))
        l_ref[g] = lsum
        pv = _dot(jnp.concatenate(ps, axis=1), v_ref[pl.ds(start, t), g * dv:(g + 1) * dv])
        acc_ref[g] = pv if first else acc_ref[g] + pv

    for g in range(group):
        mb = jnp.broadcast_to(jnp.max(m_ref[g], axis=-1, keepdims=True), (2 * t, V7X_LANES))
        m_ref[g] = mb
        pass2_block(g, i, mb, True)

    def pass2(j, carry):
        for g in range(group):
            pass2_block(g, j, m_ref[g], False)
        return carry

    lax.fori_loop(0, i, pass2, 0)

    lam = _diff_lambda(lq1_ref, lk1_ref, lq2_ref, lk2_ref, lam_init)
    for g in range(group):
        o = acc_ref[g] / jnp.sum(l_ref[g], axis=-1, keepdims=True)
        o = o[:t] - lam * o[t:]
        o_ref[:, g * dv:(g + 1) * dv] = (_rms(o, subln_ref[...]) * (1.0 - lam_init)).astype(o_ref.dtype)


def diff_attn_prompt(q, k, v, slopes, lparams, subln, *, batch, seq, n_heads, dh, dv, lam_init):
    t = _pick(seq, (512, 256, 128))
    nq = seq // t
    group = _pick(n_heads, (4, 2, 1))
    vec = lambda n: pl.BlockSpec((1, n), lambda b, h, i: (0, 0))
    return pl.pallas_call(
        functools.partial(_diff_attn_kernel, t=t, dh=dh, dv=dv, group=group, lam_init=lam_init),
        grid=(batch, n_heads // group, nq),
        in_specs=[pl.BlockSpec(memory_space=pltpu.SMEM),
                  pl.BlockSpec((t, group * 2 * dh), lambda b, h, i: (b * nq + i, h)),
                  pl.BlockSpec((seq, group * 2 * dh), lambda b, h, i: (b, h)),
                  pl.BlockSpec((seq, group * dv), lambda b, h, i: (b, h)),
                  vec(dh), vec(dh), vec(dh), vec(dh), vec(dv)],
        out_specs=pl.BlockSpec((t, group * dv), lambda b, h, i: (b * nq + i, h)),
        out_shape=jax.ShapeDtypeStruct((batch * seq, n_heads * dv), BF16),
        scratch_shapes=[pltpu.VMEM((group, 2 * t, seq), F32),
                        pltpu.VMEM((group, 2 * t, V7X_LANES), F32),
                        pltpu.VMEM((group, 2 * t, V7X_LANES), F32),
                        pltpu.VMEM((group, 2 * t, dv), F32)],
        compiler_params=_params("parallel", "parallel", "arbitrary"),
        name="diff_attn_prompt",
    )(slopes, q, k, v, *lparams, subln)


def _retention_kernel(q_ref, k_ref, v_ref, g_ref, norm_ref, y_ref, s_ref, *, c, n_heads, dk, dv,
                      log_gammas):
    step = pl.program_id(1)

    @pl.when(step == 0)
    def _():
        s_ref[...] = jnp.zeros_like(s_ref)

    row = lax.broadcasted_iota(jnp.int32, (c, c), 0)
    col = lax.broadcasted_iota(jnp.int32, (c, c), 1)
    diff = (row - col).astype(F32)
    n_v = lax.broadcasted_iota(jnp.int32, (c, dv), 0).astype(F32)
    n_k = lax.broadcasted_iota(jnp.int32, (c, dk), 0).astype(F32)
    scale = dk ** -0.5
    for h in range(n_heads):
        lg = log_gammas[h]
        q = q_ref[:, h * dk:(h + 1) * dk]
        k = k_ref[:, h * dk:(h + 1) * dk]
        v = v_ref[:, h * dv:(h + 1) * dv]
        state = s_ref[h]
        dmask = jnp.where(diff >= 0, jnp.exp(lg * jnp.maximum(diff, 0.0)), 0.0) * scale
        inner = _dot_nt(q, k) * dmask
        o = _dot(inner.astype(BF16), v)
        o = o + _dot(q, state.astype(BF16)) * jnp.exp(lg * (n_v + 1.0))
        k_dec = (k.astype(F32) * (jnp.exp(lg * (c - 1.0 - n_k)) * scale)).T
        s_ref[h] = math.exp(lg * c) * state + _dot(k_dec.astype(BF16), v)
        gate = g_ref[:, h * dv:(h + 1) * dv]
        y = _rms(o, norm_ref[...]) * (gate * jax.nn.sigmoid(gate))
        y_ref[:, h * dv:(h + 1) * dv] = y.astype(y_ref.dtype)


def retention_prompt(pa, g_r, ret_norm, *, batch, seq, n_heads, dk, dv, chunk, log_gammas,
                     q_col, k_col, v_col):
    nc = seq // chunk
    wq = n_heads * dk
    wv = n_heads * dv
    return pl.pallas_call(
        functools.partial(_retention_kernel, c=chunk, n_heads=n_heads, dk=dk, dv=dv,
                          log_gammas=log_gammas),
        grid=(batch, nc),
        in_specs=[pl.BlockSpec((chunk, wq), lambda b, s: (b * nc + s, q_col // wq)),
                  pl.BlockSpec((chunk, wq), lambda b, s: (b * nc + s, k_col // wq)),
                  pl.BlockSpec((chunk, wv), lambda b, s: (b * nc + s, v_col // wv)),
                  pl.BlockSpec((chunk, wv), lambda b, s: (b * nc + s, 0)),
                  pl.BlockSpec((1, dv), lambda b, s: (0, 0))],
        out_specs=(pl.BlockSpec((chunk, wv), lambda b, s: (b * nc + s, 0)),
                   pl.BlockSpec((None, n_heads, dk, dv), lambda b, s: (b, 0, 0, 0))),
        out_shape=(jax.ShapeDtypeStruct((batch * seq, wv), BF16),
                   jax.ShapeDtypeStruct((batch, n_heads, dk, dv), F32)),
        compiler_params=_params("parallel", "arbitrary"),
        name="retention_prompt",
    )(pa, pa, pa, g_r, ret_norm.reshape(1, dv))


def _cross_block_kernel(x_ref, g_ref, wq_ref, k_ref, v_ref, wo_ref, o_ref, *, n_heads, dh):
    x = x_ref[...]
    q = _dot(_rms(x, g_ref[...]).astype(BF16), wq_ref[...].astype(BF16)).astype(BF16)
    heads = []
    for h in range(n_heads):
        sl = slice(h * dh, (h + 1) * dh)
        s = _dot_nt(q[:, sl], k_ref[:, sl]) * (dh ** -0.5)
        m = jnp.max(s, axis=-1, keepdims=True)
        p = jnp.exp(s - m)
        p = p / jnp.sum(p, axis=-1, keepdims=True)
        heads.append(_dot(p.astype(BF16), v_ref[:, sl]).astype(BF16))
    o_ref[...] = x + _dot(jnp.concatenate(heads, axis=1), wo_ref[...].astype(BF16))


def cross_block_prompt(x, g, wq, mkv, wo, layer, *, batch, seq, n_mem, n_heads, dh):
    d = x.shape[1]
    w = n_heads * dh
    tq = _pick(seq, (1024, 512, 256, 128))
    nq = seq // tq
    row_blk = pl.BlockSpec((tq, d), lambda b, i: (b * nq + i, 0))
    return pl.pallas_call(
        functools.partial(_cross_block_kernel, n_heads=n_heads, dh=dh),
        grid=(batch, nq),
        in_specs=[row_blk,
                  pl.BlockSpec((1, d), lambda b, i: (0, 0)),
                  pl.BlockSpec((None, d, w), lambda b, i: (layer, 0, 0)),
                  pl.BlockSpec((n_mem, w), lambda b, i: (b, 0)),
                  pl.BlockSpec((n_mem, w), lambda b, i: (b, 1)),
                  pl.BlockSpec((None, w, d), lambda b, i: (layer, 0, 0))],
        out_specs=row_blk,
        out_shape=jax.ShapeDtypeStruct(x.shape, F32),
        compiler_params=_params("parallel", "parallel"),
        name="cross_block_prompt",
    )(x, g.reshape(1, d), wq, mkv, mkv, wo)


def _diff_decode_kernel(*refs, n_pages, page, n_heads, dh, lam_init):
    pos = 1
    qt_ref, knt_ref, vn_ref = refs[pos:pos + 3]; pos += 3
    kt_refs = refs[pos:pos + n_pages]; pos += n_pages
    v_refs = refs[pos:pos + n_pages]; pos += n_pages
    slopes_ref = refs[pos]; pos += 1
    lparams = refs[pos:pos + 4]; pos += 4
    subln_ref = refs[pos]; pos += 1
    o_ref = refs[pos]

    b = pl.program_id(0)
    nb = qt_ref.shape[1]
    onehot = (lax.broadcasted_iota(jnp.int32, (nb, V7X_LANES), 0) == b).astype(BF16)
    q_rep = _dot(qt_ref[...], onehot) * (dh ** -0.5)
    kn_rep = _dot(knt_ref[...], onehot)

    def map_sums(x):
        x3 = x.reshape(n_heads, 2 * dh, x.shape[-1])
        return x3[:, :dh, :].sum(axis=1), x3[:, dh:, :].sum(axis=1)

    past = n_pages * page
    tpos = lax.broadcasted_iota(jnp.int32, (n_heads, page), 1).astype(F32)
    nslope = -slopes_ref[...]
    s1, s2 = [], []
    for p in range(n_pages):
        a, c = map_sums(kt_refs[p][...] * q_rep)
        bias = nslope * ((past - p * page) - tpos)
        s1.append(a + bias)
        s2.append(c + bias)
    n1, n2 = map_sums(kn_rep * q_rep)

    def softmax_parts(s_list, s_new):
        m = s_list[0]
        for s in s_list[1:]:
            m = jnp.maximum(m, s)
        m = jnp.maximum(jnp.max(m, axis=-1, keepdims=True), s_new)
        e = [jnp.exp(s - m) for s in s_list]
        e_new = jnp.exp(s_new - m)
        l = e[0]
        for x in e[1:]:
            l = l + x
        l = jnp.sum(l, axis=-1, keepdims=True) + e_new
        return e, e_new, 1.0 / l

    e1, en1, a1 = softmax_parts(s1, n1)
    e2, en2, a2 = softmax_parts(s2, n2)
    a2 = a2 * _diff_lambda(*lparams, lam_init)

    tok = lax.broadcasted_iota(jnp.int32, (page, n_heads, V7X_LANES), 0)
    lane = lax.broadcasted_iota(jnp.int32, (page, n_heads, V7X_LANES), 2)
    diag = tok == lane
    ones = jnp.ones((V7X_LANES, V7X_LANES), BF16)
    o = (en1 * a1 - en2 * a2) * vn_ref[...]
    for p in range(n_pages):
        w = e1[p] * a1 - e2[p] * a2
        wd = jnp.where(diag, w[None], 0.0).reshape(page * n_heads, V7X_LANES).astype(BF16)
        w3 = _dot(wd, ones).reshape(page, n_heads, V7X_LANES)
        o = o + jnp.sum(w3 * v_refs[p][...], axis=0)
    o_ref[...] = (_rms(o, subln_ref[...]) * (1.0 - lam_init)).astype(o_ref.dtype)


def diff_attn_sample(q, k_new, v_new, cache_kt, cache_v, layer, page_table, slopes, lparams, subln, *,
                     n_heads, dh, dv, lam_init):
    nb, n_pages = page_table.shape
    page = cache_v.shape[2]
    wk = n_heads * 2 * dh
    assert page == V7X_LANES and dv == V7X_LANES and n_heads == V7X_SUBLANES
    const = lambda shape: pl.BlockSpec(shape, lambda b, pt: (0,) * len(shape))

    def kt_spec(p):
        return pl.BlockSpec((None, None, wk, page), lambda b, pt: (layer, pt[b, p], 0, 0))

    def v_spec(p):
        return pl.BlockSpec((None, None, page, n_heads, dv), lambda b, pt: (layer, pt[b, p], 0, 0, 0))

    in_specs = ([const((wk, nb)), const((wk, nb)),
                 pl.BlockSpec((None, n_heads, dv), lambda b, pt: (b, 0, 0))]
                + [kt_spec(p) for p in range(n_pages)] + [v_spec(p) for p in range(n_pages)]
                + [const((n_heads, V7X_LANES))] + [const((1, dh))] * 4 + [const((1, dv))])
    out = pl.pallas_call(
        functools.partial(_diff_decode_kernel, n_pages=n_pages, page=page, n_heads=n_heads, dh=dh,
                          lam_init=lam_init),
        grid_spec=pltpu.PrefetchScalarGridSpec(
            num_scalar_prefetch=1, grid=(nb,), in_specs=in_specs,
            out_specs=pl.BlockSpec((None, n_heads, dv), lambda b, pt: (b, 0, 0))),
        out_shape=jax.ShapeDtypeStruct((nb, n_heads, dv), BF16),
        compiler_params=_params("arbitrary"),
        name="diff_attn_sample",
    )(page_table, q.T.astype(BF16), k_new.T.astype(BF16), v_new.reshape(nb, n_heads, dv),
      *([cache_kt] * n_pages), *([cache_v] * n_pages),
      jnp.broadcast_to(slopes[:, None], (n_heads, V7X_LANES)), *lparams, subln)
    return out.reshape(nb, n_heads * dv)


def _cross_decode_kernel(q_ref, k_ref, v_ref, o_ref, *, n_heads, dh):
    n_req, n_tiles = k_ref.shape[:2]
    ones = jnp.ones((dh, V7X_LANES), BF16)

    def over_tokens(x, op):
        out = x
        for r in range(n_heads, V7X_SUBLANES, n_heads):
            out = op(out, pltpu.roll(x, r, axis=0))
        return out

    for r in range(n_req):
        q = q_ref[r] * (dh ** -0.5)
        prod = (k_ref[r] * q[None]).reshape(n_tiles * V7X_SUBLANES, dh).astype(BF16)
        s = _dot(prod, ones).reshape(n_tiles, V7X_SUBLANES, V7X_LANES)
        m = over_tokens(jnp.max(s, axis=0), jnp.maximum)
        e = jnp.exp(s - m[None])
        l = over_tokens(jnp.sum(e, axis=0), jnp.add)
        o = over_tokens(jnp.sum(e * v_ref[r], axis=0), jnp.add)
        o_ref[r] = o / l


def cross_attn_sample(q, cache_k, cache_v, layer, *, n_heads, dh):
    nb = q.shape[0]
    assert dh == V7X_LANES and V7X_SUBLANES % n_heads == 0
    n_tiles = cache_k.shape[2]
    q8 = jnp.tile(q.reshape(nb, n_heads, dh), (1, V7X_SUBLANES // n_heads, 1))
    rb = _pick(nb, (4, 2, 1))
    kv_spec = pl.BlockSpec((None, rb, n_tiles, V7X_SUBLANES, dh), lambda b: (layer, b, 0, 0, 0))
    row = pl.BlockSpec((rb, V7X_SUBLANES, dh), lambda b: (b, 0, 0))
    out = pl.pallas_call(
        functools.partial(_cross_decode_kernel, n_heads=n_heads, dh=dh),
        grid=(nb // rb,),
        in_specs=[row, kv_spec, kv_spec],
        out_specs=row,
        out_shape=jax.ShapeDtypeStruct((nb, V7X_SUBLANES, dh), F32),
        compiler_params=_params("parallel"),
        name="cross_attn_sample",
    )(q8, cache_k, cache_v)
    return out[:, :n_heads].reshape(nb, n_heads * dh)


def _retention_step_kernel(q_ref, k_ref, v_ref, s_ref, *rest, n_heads, dk, dv, log_gammas):
    o_ref, sn_ref = rest[-2:]
    scale = dk ** -0.5
    eye = (lax.broadcasted_iota(jnp.int32, (dk, dk), 0)
           == lax.broadcasted_iota(jnp.int32, (dk, dk), 1))
    for r in range(q_ref.shape[0]):
        for h in range(n_heads):
            gamma = math.exp(log_gammas[h])
            q = q_ref[r, :, h * dk:(h + 1) * dk]
            k = k_ref[r, :, h * dk:(h + 1) * dk]
            v = v_ref[r, :, h * dv:(h + 1) * dv]
            state = s_ref[r, h]
            inner = jnp.sum(q * k, axis=-1, keepdims=True) * scale
            q8 = jnp.broadcast_to(q, (V7X_SUBLANES, dk)).astype(BF16)
            o_cross = _dot(q8, state.astype(BF16))[0:1] * gamma
            o_ref[r, :, h * dv:(h + 1) * dv] = inner * v + o_cross
            k_diag = jnp.where(eye, jnp.broadcast_to(k * scale, (dk, dk)), 0.0).astype(BF16)
            v_rows = jnp.broadcast_to(v, (dk, dv)).astype(BF16)
            sn_ref[r, h] = gamma * state + _dot(k_diag, v_rows)


def retention_sample(q, k, v, state, layer, stacked, *, n_heads, dk, dv, log_gammas):
    nb = q.shape[0]
    wq, wv = n_heads * dk, n_heads * dv
    rb = _pick(nb, (4, 2, 1))
    row = lambda w: pl.BlockSpec((rb, 1, w), lambda b, *_: (b, 0, 0))
    state_blk = pl.BlockSpec((None, rb, n_heads, dk, dv), lambda b, *_: (layer, b, 0, 0, 0))
    ins = [q.reshape(nb, 1, wq), k.reshape(nb, 1, wq), v.reshape(nb, 1, wv), state]
    in_specs = [row(wq), row(wq), row(wv), state_blk]
    aliases = _stacked(in_specs, ins, stacked, n_out_before=1)
    o, s_new = pl.pallas_call(
        functools.partial(_retention_step_kernel, n_heads=n_heads, dk=dk, dv=dv,
                          log_gammas=log_gammas),
        grid=(nb // rb,),
        in_specs=in_specs,
        out_specs=(row(wv), state_blk),
        out_shape=(jax.ShapeDtypeStruct((nb, 1, wv), F32),
                   jax.ShapeDtypeStruct(state.shape, F32)),
        input_output_aliases=aliases,
        compiler_params=_params("parallel"),
        name="retention_sample",
    )(*ins)
    return o.reshape(nb, wv), s_new


def kernel(x_prompt, x_sample, mem_prompt, cache_k_diff, cache_v_diff, cache_mem_k, cache_mem_v,
           state_ret, page_table, norm_mix, w_in, w_out, da_lq1, da_lk1, da_lq2, da_lk2, da_subln,
           ret_norm, norm_x, norm_mem, wx_q, wx_kv, wx_o, norm_ff, w_up, w_down, norm_final):
    batch, seq, d_model = x_prompt.shape
    dec_batch, dec_seq, _ = x_sample.shape
    assert dec_seq == 1
    depth, n_pool, page, h_da, _, dh_da = cache_k_diff.shape
    dv_da = cache_v_diff.shape[-1]
    _, _, h_ret, dk_ret, dv_ret = state_ret.shape
    _, _, n_mem, h_x, dh_x = cache_mem_k.shape
    d_qk = h_da * 2 * dh_da
    d_da = h_da * dv_da
    d_rqk = h_ret * dk_ret
    d_ret = h_ret * dv_ret
    d_x = h_x * dh_x
    chunk = _pick(seq, (256, 128))
    mp = batch * seq

    slopes = 2.0 ** (-8.0 * (jnp.arange(h_da, dtype=F32) + 1.0) / h_da)
    log_gammas = tuple(math.log(1.0 - 2.0 ** (-5.0 - h)) for h in range(h_ret))

    o_q, o_k, o_v = 0, d_qk, 2 * d_qk
    o_qr = o_v + d_da
    o_kr, o_vr = o_qr + d_rqk, o_qr + 2 * d_rqk
    o_g = o_vr + d_ret

    cache_kt = jnp.transpose(cache_k_diff, (0, 1, 3, 4, 5, 2)).reshape(depth, n_pool, d_qk, page)
    mem_tiles = n_mem * h_x // V7X_SUBLANES
    mem_k = cache_mem_k.reshape(depth, dec_batch, mem_tiles, V7X_SUBLANES, dh_x)
    mem_v = cache_mem_v.reshape(depth, dec_batch, mem_tiles, V7X_SUBLANES, dh_x)

    xp = x_prompt.reshape(mp, d_model)
    xs = x_sample.reshape(dec_batch, d_model)
    mem = mem_prompt.reshape(batch * n_mem, d_model)

    wu_all, wd_all = w_up.astype(BF16), w_down.astype(BF16)
    outs = {name: [] for name in ("ps", "pmk", "pmv", "sk", "sv")}
    pk_all = pv_all = ss_all = None
    for l in range(depth):
        lam_init = 0.8 - 0.6 * math.exp(-0.3 * l)
        lparams = tuple(a[l].reshape(1, dh_da) for a in (da_lq1, da_lk1, da_lq2, da_lk2))
        subln = da_subln[l].reshape(1, dv_da)
        seg = lambda c0, c1: ((c0, c1 - c0),)
        w_a = WView(w_in, l, 0, d_model, seg(o_q, o_k) + seg(o_qr, o_g))
        w_k = WView(w_in, l, 0, d_model, seg(o_k, o_v))
        w_v = WView(w_in, l, 0, d_model, seg(o_v, o_qr))
        w_g = WView(w_in, l, 0, d_model, seg(o_g, o_g + d_ret))
        w_out_da = WView(w_out, l, 0, d_da, seg(0, d_model))
        w_out_r = WView(w_out, l, d_da, d_ret, seg(0, d_model))
        wq_b = WView(wx_q, l, 0, d_model, seg(0, d_x))
        wkv_b = WView(wx_kv, l, 0, d_model, seg(0, 2 * d_x))
        wo_b = WView(wx_o, l, 0, d_x, seg(0, d_model))
        a_q, a_qr, a_kr, a_vr = 0, d_qk, d_qk + d_rqk, d_qk + 2 * d_rqk

        last = l == depth - 1
        g_next = norm_final if last else norm_mix[l + 1]

        if l == 0:
            pk_all, k_b, h = kproj(xp, w_k, l, depth, None, batch=batch, seq=seq, norm_gain=norm_mix[l])
        else:
            h = hp_next
            pk_all, k_b = kproj(h, w_k, l, depth, pk_all, batch=batch, seq=seq)
        (pa,) = matmul([(h, w_a)], (BF16,))
        pv_all, v_b = vproj(h, w_v, l, depth, pv_all, n_heads=h_da)
        (g_r,) = matmul([(h, w_g)], (F32,))
        y_da = diff_attn_prompt(pa, k_b, v_b, slopes, lparams, subln,
                                batch=batch, seq=seq, n_heads=h_da, dh=dh_da, dv=dv_da, lam_init=lam_init)
        y_r, s_p = retention_prompt(pa, g_r, ret_norm[l], batch=batch, seq=seq, n_heads=h_ret,
                                    dk=dk_ret, dv=dv_ret, chunk=chunk, log_gammas=log_gammas,
                                    q_col=a_qr, k_col=a_kr, v_col=a_vr)
        (xp,) = matmul([(y_da, w_out_da), (y_r, w_out_r)], (F32,), residual=xp)
        hm = rmsnorm(mem, norm_mem[l], BF16)
        mkv_f32, mkv_b = matmul([(hm, wkv_b)], (F32, BF16))
        xp = cross_block_prompt(xp, norm_x[l], wx_q, mkv_b, wx_o, l, batch=batch, seq=seq, n_mem=n_mem,
                                n_heads=h_x, dh=dh_x)
        if last:
            (y_prompt,) = mlp(xp, norm_ff[l], wu_all, wd_all, l, g_next, keep_x=False)
        else:
            xp, hp_next = mlp(xp, norm_ff[l], wu_all, wd_all, l, g_next, keep_x=True)
        outs["ps"].append(s_p)
        outs["pmk"].append(mkv_f32[:, :d_x].reshape(batch, n_mem, h_x, dh_x))
        outs["pmv"].append(mkv_f32[:, d_x:].reshape(batch, n_mem, h_x, dh_x))

        h = rmsnorm(xs, norm_mix[l], BF16) if l == 0 else hs_next
        (ps_a,) = matmul([(h, w_a)], (F32,))
        (k_new,) = matmul([(h, w_k)], (F32,))
        (v_new,) = matmul([(h, w_v)], (F32,))
        (g_s,) = matmul([(h, w_g)], (F32,))
        y_da = diff_attn_sample(ps_a[:, a_q:a_qr], k_new, v_new, cache_kt, cache_v_diff, l, page_table,
                                slopes, lparams, subln, n_heads=h_da, dh=dh_da, dv=dv_da,
                                lam_init=lam_init)
        o_r, ss_all = retention_sample(ps_a[:, a_qr:a_kr], ps_a[:, a_kr:a_vr], ps_a[:, a_vr:], state_ret, l, ss_all,
                                    n_heads=h_ret, dk=dk_ret, dv=dv_ret, log_gammas=log_gammas)
        y_r = gated_headnorm(o_r.reshape(dec_batch * h_ret, dv_ret), ret_norm[l],
                             g_s.reshape(dec_batch * h_ret, dv_ret))
        (xs,) = matmul([(y_da, w_out_da), (y_r.reshape(dec_batch, d_ret), w_out_r)],
                       (F32,), residual=xs)
        hx = rmsnorm(xs, norm_x[l], BF16)
        (qx,) = matmul([(hx, wq_b)], (F32,))
        ox = cross_attn_sample(qx, mem_k, mem_v, l, n_heads=h_x, dh=dh_x)
        (xs,) = matmul([(ox.astype(BF16), wo_b)], (F32,), residual=xs)
        if last:
            (y_sample,) = mlp(xs, norm_ff[l], wu_all, wd_all, l, g_next, keep_x=False)
        else:
            xs, hs_next = mlp(xs, norm_ff[l], wu_all, wd_all, l, g_next, keep_x=True)
        outs["sk"].append(k_new.reshape(dec_batch, 1, h_da, 2, dh_da))
        outs["sv"].append(v_new.reshape(dec_batch, 1, h_da, dv_da))

    y_prompt = y_prompt.reshape(batch, seq, d_model)
    y_sample = y_sample.reshape(dec_batch, 1, d_model)
    st = lambda name: jnp.stack(outs[name])
    new_k_prompt = jnp.transpose(pk_all.reshape(depth, batch, h_da, 2, dh_da, seq), (0, 1, 5, 2, 3, 4))
    return (y_prompt, y_sample, new_k_prompt, pv_all.reshape(depth, batch, seq, h_da, dv_da), st("ps"),
            st("pmk"), st("pmv"), st("sk"), st("sv"), ss_all)
```

```python
import functools
import math
from typing import NamedTuple

import jax
import jax.numpy as jnp
from jax import lax
from jax.experimental import pallas as pl
from jax.experimental.pallas import tpu as pltpu

EPS = 1e-5
NEG_INF = -1e30
LOG2E = math.log2(math.e)
F32 = jnp.float32
BF16 = jnp.bfloat16

V7X_VMEM_LIMIT_BYTES = 60 * 1024 * 1024
V7X_LANES = 128
V7X_SUBLANES = 8


def _params(*semantics):
    return pltpu.CompilerParams(dimension_semantics=semantics,
                                vmem_limit_bytes=V7X_VMEM_LIMIT_BYTES)


def _pick(n, candidates):
    for c in candidates:
        if n % c == 0:
            return c
    return n


def _dot(a, b):
    return jnp.dot(a, b, preferred_element_type=F32)


def _dot_nt(a, b):
    return lax.dot_general(a, b, (((1,), (1,)), ((), ())), preferred_element_type=F32)


class WView(NamedTuple):
    arr: jax.Array
    layer: int
    row0: int
    k: int
    segs: tuple

    @property
    def n(self):
        return sum(width for _, width in self.segs)

    def spec(self, tn, grid_rank):
        assert self.row0 % self.k == 0 and all(c % tn == 0 and w % tn == 0 for c, w in self.segs)
        blocks = [c // tn + t for c, w in self.segs for t in range(w // tn)]
        layer, row_blk = self.layer, self.row0 // self.k

        def col(j):
            blk = blocks[-1]
            for idx in range(len(blocks) - 2, -1, -1):
                blk = jnp.where(j == idx, blocks[idx], blk)
            return blk

        if len(blocks) == 1:
            return pl.BlockSpec((None, self.k, tn), lambda *g: (layer, row_blk, blocks[0]))
        assert grid_rank == 2
        return pl.BlockSpec((None, self.k, tn), lambda j, i: (layer, row_blk, col(j)))


def _rms(x, g):
    ms = jnp.mean(x * x, axis=-1, keepdims=True)
    return x * lax.rsqrt(ms + EPS) * g


def _rmsnorm_kernel(x_ref, g_ref, o_ref):
    o_ref[...] = _rms(x_ref[...], g_ref[...]).astype(o_ref.dtype)


def rmsnorm(x, g, out_dtype):
    m, d = x.shape
    tm = _pick(m, (512, 256, 128))
    return pl.pallas_call(
        _rmsnorm_kernel,
        grid=(m // tm,),
        in_specs=[pl.BlockSpec((tm, d), lambda i: (i, 0)),
                  pl.BlockSpec((1, d), lambda i: (0, 0))],
        out_specs=pl.BlockSpec((tm, d), lambda i: (i, 0)),
        out_shape=jax.ShapeDtypeStruct((m, d), out_dtype),
        compiler_params=_params("parallel"),
        name="rmsnorm",
    )(x, g.reshape(1, d))


def _gated_headnorm_kernel(x_ref, g_ref, gate_ref, o_ref):
    gate = gate_ref[...]
    y = _rms(x_ref[...], g_ref[...]) * (gate * jax.nn.sigmoid(gate))
    o_ref[...] = y.astype(o_ref.dtype)


def gated_headnorm(x, g, gate):
    r, d = x.shape
    tr = _pick(r, (512, 256, 128))
    blk = pl.BlockSpec((tr, d), lambda i: (i, 0))
    return pl.pallas_call(
        _gated_headnorm_kernel,
        grid=(r // tr,),
        in_specs=[blk, pl.BlockSpec((1, d), lambda i: (0, 0)), blk],
        out_specs=blk,
        out_shape=jax.ShapeDtypeStruct((r, d), BF16),
        compiler_params=_params("parallel"),
        name="gated_headnorm",
    )(x, g.reshape(1, d), gate)


def _mm_kernel(*refs, n_pairs, has_res):
    acc = _dot(refs[0][...], refs[1][...].astype(BF16))
    for p in range(1, n_pairs):
        acc = acc + _dot(refs[2 * p][...], refs[2 * p + 1][...].astype(BF16))
    pos = 2 * n_pairs
    if has_res:
        acc = refs[pos][...] + acc
        pos += 1
    for o_ref in refs[pos:]:
        o_ref[...] = acc.astype(o_ref.dtype)


def matmul(pairs, out_dtypes, residual=None):
    m = pairs[0][0].shape[0]
    n = pairs[0][1].n
    tm = _pick(m, (1024, 512, 256, 128))
    tn = _pick(n, (1024, 512, 256, 128))
    ins, specs = [], []
    for a, w in pairs:
        assert a.shape[1] == w.k and w.n == n
        ins += [a, w.arr]
        specs += [pl.BlockSpec((tm, w.k), lambda j, i: (i, 0)), w.spec(tn, 2)]
    if residual is not None:
        ins.append(residual)
        specs.append(pl.BlockSpec((tm, tn), lambda j, i: (i, j)))
    return pl.pallas_call(
        functools.partial(_mm_kernel, n_pairs=len(pairs), has_res=residual is not None),
        grid=(n // tn, m // tm),
        in_specs=specs,
        out_specs=tuple(pl.BlockSpec((tm, tn), lambda j, i: (i, j)) for _ in out_dtypes),
        out_shape=tuple(jax.ShapeDtypeStruct((m, n), dt) for dt in out_dtypes),
        compiler_params=_params("parallel", "parallel"),
        name="matmul",
    )(*ins)


def _vproj_kernel(a_ref, w_ref, *rest, n_heads):
    o4_ref, ob_ref = rest[-2:]
    acc = _dot(a_ref[...], w_ref[...].astype(BF16))
    ob_ref[...] = acc.astype(BF16)
    dv = acc.shape[1] // n_heads
    for h in range(n_heads):
        o4_ref[:, h, :] = acc[:, h * dv:(h + 1) * dv]


def vproj(a, w, layer, depth, stacked, *, n_heads):
    m, k = a.shape
    n = w.n
    dv = n // n_heads
    tm = _pick(m, (512, 256, 128))
    ins = [a, w.arr]
    in_specs = [pl.BlockSpec((tm, k), lambda i: (i, 0)), w.spec(n, 1)]
    aliases = _stacked(in_specs, ins, stacked)
    return pl.pallas_call(
        functools.partial(_vproj_kernel, n_heads=n_heads),
        grid=(m // tm,),
        in_specs=in_specs,
        out_specs=(pl.BlockSpec((None, tm, n_heads, dv), lambda i: (layer, i, 0, 0)),
                   pl.BlockSpec((tm, n), lambda i: (i, 0))),
        out_shape=(jax.ShapeDtypeStruct((depth, m, n_heads, dv), F32),
                   jax.ShapeDtypeStruct((m, n), BF16)),
        input_output_aliases=aliases,
        compiler_params=_params("parallel"),
        name="vproj",
    )(*ins)


def _kproj_kernel(*refs, norm):
    if norm:
        x_ref, g_ref, w_ref = refs[:3]
        ot_ref, ob_ref, hn_ref = refs[-3:]
        h = _rms(x_ref[...], g_ref[...]).astype(BF16)
        hn_ref[...] = h
    else:
        h_ref, w_ref = refs[:2]
        ot_ref, ob_ref = refs[-2:]
        h = h_ref[...]
    k = _dot(h, w_ref[...].astype(BF16))
    ob_ref[...] = k.astype(BF16)
    ot_ref[...] = k.T


def kproj(h, w, layer, depth, stacked, *, batch, seq, norm_gain=None):
    d = h.shape[1]
    n = w.n
    ts = _pick(seq, (1024, 512, 256, 128))
    ns = seq // ts
    norm = norm_gain is not None
    assert not (norm and stacked is not None)
    row_blk = lambda width: pl.BlockSpec((ts, width), lambda b, i: (b * ns + i, 0))
    ins = [h] + ([norm_gain.reshape(1, d)] if norm else []) + [w.arr]
    in_specs = [row_blk(d)] + ([pl.BlockSpec((1, d), lambda b, i: (0, 0))] if norm else []) + [w.spec(n, 1)]
    aliases = _stacked(in_specs, ins, stacked)
    out_specs = [pl.BlockSpec((None, None, n, ts), lambda b, i: (layer, b, 0, i)), row_blk(n)]
    out_shape = [jax.ShapeDtypeStruct((depth, batch, n, seq), F32),
                 jax.ShapeDtypeStruct((batch * seq, n), BF16)]
    if norm:
        out_specs.append(row_blk(d))
        out_shape.append(jax.ShapeDtypeStruct((batch * seq, d), BF16))
    return pl.pallas_call(
        functools.partial(_kproj_kernel, norm=norm),
        grid=(batch, ns),
        in_specs=in_specs,
        out_specs=tuple(out_specs),
        out_shape=tuple(out_shape),
        input_output_aliases=aliases,
        compiler_params=_params("parallel", "parallel"),
        name="kproj",
    )(*ins)


def _stacked(spec_list, ins, stacked, n_out_before=0):
    if stacked is None:
        return {}
    ins.append(stacked)
    spec_list.append(pl.BlockSpec(memory_space=pl.ANY))
    return {len(ins) - 1: n_out_before}


def _mlp_kernel(x_ref, g_ref, wu_ref, wd_ref, gn_ref, *refs, keep_x):
    acc_ref, h_ref = refs[0], refs[-1]
    f = pl.program_id(1)

    @pl.when(f == 0)
    def _():
        h_ref[...] = _rms(x_ref[...], g_ref[...]).astype(BF16)

    u = _dot(h_ref[...], wu_ref[...])
    a = jnp.square(jnp.maximum(u, 0.0)).astype(BF16)
    c = _dot(a, wd_ref[...])

    @pl.when(f == 0)
    def _():
        acc_ref[...] = x_ref[...] + c

    @pl.when(f != 0)
    def _():
        acc_ref[...] += c

    @pl.when(f == pl.num_programs(1) - 1)
    def _():
        n_ref = refs[1] if keep_x else acc_ref
        n_ref[...] = _rms(acc_ref[...], gn_ref[...]).astype(n_ref.dtype)


def mlp(x, g, w_up, w_down, layer, g_next, *, keep_x):
    m, d = x.shape
    ff = w_up.shape[2]
    tm = _pick(m, (512, 256, 128))
    tf = _pick(ff, (2048, 1024, 512, 256, 128))
    row_blk = pl.BlockSpec((tm, d), lambda i, f: (i, 0))
    vec = pl.BlockSpec((1, d), lambda i, f: (0, 0))
    out_shape = [jax.ShapeDtypeStruct((m, d), F32)]
    if keep_x:
        out_shape.append(jax.ShapeDtypeStruct((m, d), BF16))
    return pl.pallas_call(
        functools.partial(_mlp_kernel, keep_x=keep_x),
        grid=(m // tm, ff // tf),
        in_specs=[row_blk, vec,
                  pl.BlockSpec((None, d, tf), lambda i, f: (layer, 0, f)),
                  pl.BlockSpec((None, tf, d), lambda i, f: (layer, f, 0)),
                  vec],
        out_specs=tuple(row_blk for _ in out_shape),
        out_shape=tuple(out_shape),
        scratch_shapes=[pltpu.VMEM((tm, d), BF16)],
        compiler_params=_params("parallel", "arbitrary"),
        name="mlp",
    )(x, g.reshape(1, d), w_up, w_down, g_next.reshape(1, d))


def _diff_lambda(lq1_ref, lk1_ref, lq2_ref, lk2_ref, lam_init):
    e1 = jnp.exp(jnp.sum(lq1_ref[...] * lk1_ref[...], axis=-1, keepdims=True))
    e2 = jnp.exp(jnp.sum(lq2_ref[...] * lk2_ref[...], axis=-1, keepdims=True))
    return e1 - e2 + lam_init


def _lane_fold(x, op):
    out = x[:, :V7X_LANES]
    for c in range(V7X_LANES, x.shape[1], V7X_LANES):
        out = op(out, x[:, c:c + V7X_LANES])
    return out


def _diff_attn_kernel(slopes_ref, q_ref, k_ref, v_ref, lq1_ref, lk1_ref, lq2_ref, lk2_ref,
                      subln_ref, o_ref, s_ref, m_ref, l_ref, acc_ref, *, t, dh, dv, group, lam_init):
    hg = pl.program_id(1)
    i = pl.program_id(2)
    kpos0 = lax.broadcasted_iota(jnp.int32, (1, t), 1).astype(F32)
    lane = lax.broadcasted_iota(jnp.int32, (t, 2 * dh), 1)

    qq, slope2 = [], []
    for g in range(group):
        q = q_ref[:, g * 2 * dh:(g + 1) * 2 * dh].astype(F32) * (dh ** -0.5 * LOG2E)
        qq.append(jnp.concatenate([jnp.where(lane < dh, q, 0.0), jnp.where(lane >= dh, q, 0.0)],
                                  axis=0).astype(BF16))
        slope2.append(slopes_ref[hg * group + g] * LOG2E)

    def scores(g, j):
        start = pl.multiple_of(j * t, t)
        kpos = kpos0 + (j * t).astype(F32)
        return _dot_nt(qq[g], k_ref[pl.ds(start, t), g * 2 * dh:(g + 1) * 2 * dh]) + slope2[g] * kpos

    def pass1(j, carry):
        for g in range(group):
            s = scores(g, j)
            s_ref[g, :, pl.ds(pl.multiple_of(j * t, t), t)] = s
            m_ref[g] = jnp.maximum(m_ref[g], _lane_fold(s, jnp.maximum))
        return carry

    row = lax.broadcasted_iota(jnp.int32, (2 * t, t), 0)
    col = lax.broadcasted_iota(jnp.int32, (2 * t, t), 1)
    visible = jnp.where(row >= t, row - t, row) >= col
    for g in range(group):
        s = jnp.where(visible, scores(g, i), NEG_INF)
        s_ref[g, :, pl.ds(pl.multiple_of(i * t, t), t)] = s
        m_ref[g] = _lane_fold(s, jnp.maximum)
    lax.fori_loop(0, i, pass1, 0)

    def pass2_block(g, j, mb, first):
        start = pl.multiple_of(j * t, t)
        ps = []
        lsum = None if first else l_ref[g]
        for c in range(0, t, V7X_LANES):
            p = jnp.exp2(s_ref[g, :, pl.ds(start + c, V7X_LANES)] - mb)
            lsum = p if lsum is None else lsum + p
            ps.append(p.astype(BF16))
        l_ref[g] = lsum
        pv = _dot(jnp.concatenate(ps, axis=1), v_ref[pl.ds(start, t), g * dv:(g + 1) * dv])
        acc_ref[g] = pv if first else acc_ref[g] + pv

    for g in range(group):
        mb = jnp.broadcast_to(jnp.max(m_ref[g], axis=-1, keepdims=True), (2 * t, V7X_LANES))
        m_ref[g] = mb
        pass2_block(g, i, mb, True)

    def pass2(j, carry):
        for g in range(group):
            pass2_block(g, j, m_ref[g], False)
        return carry

    lax.fori_loop(0, i, pass2, 0)

    lam = _diff_lambda(lq1_ref, lk1_ref, lq2_ref, lk2_ref, lam_init)
    for g in range(group):
        o = acc_ref[g] / jnp.sum(l_ref[g], axis=-1, keepdims=True)
        o = o[:t] - lam * o[t:]
        o_ref[:, g * dv:(g + 1) * dv] = (_rms(o, subln_ref[...]) * (1.0 - lam_init)).astype(o_ref.dtype)


def diff_attn_prompt(q, k, v, slopes, lparams, subln, *, batch, seq, n_heads, dh, dv, lam_init):
    t = _pick(seq, (512, 256, 128))
    nq = seq // t
    group = _pick(n_heads, (4, 2, 1))
    vec = lambda n: pl.BlockSpec((1, n), lambda b, h, i: (0, 0))
    return pl.pallas_call(
        functools.partial(_diff_attn_kernel, t=t, dh=dh, dv=dv, group=group, lam_init=lam_init),
        grid=(batch, n_heads // group, nq),
        in_specs=[pl.BlockSpec(memory_space=pltpu.SMEM),
                  pl.BlockSpec((t, group * 2 * dh), lambda b, h, i: (b * nq + i, h)),
                  pl.BlockSpec((seq, group * 2 * dh), lambda b, h, i: (b, h)),
                  pl.BlockSpec((seq, group * dv), lambda b, h, i: (b, h)),
                  vec(dh), vec(dh), vec(dh), vec(dh), vec(dv)],
        out_specs=pl.BlockSpec((t, group * dv), lambda b, h, i: (b * nq + i, h)),
        out_shape=jax.ShapeDtypeStruct((batch * seq, n_heads * dv), BF16),
        scratch_shapes=[pltpu.VMEM((group, 2 * t, seq), F32),
                        pltpu.VMEM((group, 2 * t, V7X_LANES), F32),
                        pltpu.VMEM((group, 2 * t, V7X_LANES), F32),
                        pltpu.VMEM((group, 2 * t, dv), F32)],
        compiler_params=_params("parallel", "parallel", "arbitrary"),
        name="diff_attn_prompt",
    )(slopes, q, k, v, *lparams, subln)


def _retention_kernel(q_ref, k_ref, v_ref, g_ref, norm_ref, y_ref, s_ref, *, c, n_heads, dk, dv,
                      log_gammas):
    step = pl.program_id(1)

    @pl.when(step == 0)
    def _():
        s_ref[...] = jnp.zeros_like(s_ref)

    row = lax.broadcasted_iota(jnp.int32, (c, c), 0)
    col = lax.broadcasted_iota(jnp.int32, (c, c), 1)
    diff = (row - col).astype(F32)
    n_v = lax.broadcasted_iota(jnp.int32, (c, dv), 0).astype(F32)
    n_k = lax.broadcasted_iota(jnp.int32, (c, dk), 0).astype(F32)
    scale = dk ** -0.5
    for h in range(n_heads):
        lg = log_gammas[h]
        q = q_ref[:, h * dk:(h + 1) * dk]
        k = k_ref[:, h * dk:(h + 1) * dk]
        v = v_ref[:, h * dv:(h + 1) * dv]
        state = s_ref[h]
        dmask = jnp.where(diff >= 0, jnp.exp(lg * jnp.maximum(diff, 0.0)), 0.0) * scale
        inner = _dot_nt(q, k) * dmask
        o = _dot(inner.astype(BF16), v)
        o = o + _dot(q, state.astype(BF16)) * jnp.exp(lg * (n_v + 1.0))
        k_dec = (k.astype(F32) * (jnp.exp(lg * (c - 1.0 - n_k)) * scale)).T
        s_ref[h] = math.exp(lg * c) * state + _dot(k_dec.astype(BF16), v)
        gate = g_ref[:, h * dv:(h + 1) * dv]
        y = _rms(o, norm_ref[...]) * (gate * jax.nn.sigmoid(gate))
        y_ref[:, h * dv:(h + 1) * dv] = y.astype(y_ref.dtype)


def retention_prompt(pa, g_r, ret_norm, *, batch, seq, n_heads, dk, dv, chunk, log_gammas,
                     q_col, k_col, v_col):
    nc = seq // chunk
    wq = n_heads * dk
    wv = n_heads * dv
    return pl.pallas_call(
        functools.partial(_retention_kernel, c=chunk, n_heads=n_heads, dk=dk, dv=dv,
                          log_gammas=log_gammas),
        grid=(batch, nc),
        in_specs=[pl.BlockSpec((chunk, wq), lambda b, s: (b * nc + s, q_col // wq)),
                  pl.BlockSpec((chunk, wq), lambda b, s: (b * nc + s, k_col // wq)),
                  pl.BlockSpec((chunk, wv), lambda b, s: (b * nc + s, v_col // wv)),
                  pl.BlockSpec((chunk, wv), lambda b, s: (b * nc + s, 0)),
                  pl.BlockSpec((1, dv), lambda b, s: (0, 0))],
        out_specs=(pl.BlockSpec((chunk, wv), lambda b, s: (b * nc + s, 0)),
                   pl.BlockSpec((None, n_heads, dk, dv), lambda b, s: (b, 0, 0, 0))),
        out_shape=(jax.ShapeDtypeStruct((batch * seq, wv), BF16),
                   jax.ShapeDtypeStruct((batch, n_heads, dk, dv), F32)),
        compiler_params=_params("parallel", "arbitrary"),
        name="retention_prompt",
    )(pa, pa, pa, g_r, ret_norm.reshape(1, dv))


def _cross_block_kernel(x_ref, g_ref, wq_ref, k_ref, v_ref, wo_ref, o_ref, *, n_heads, dh):
    x = x_ref[...]
    q = _dot(_rms(x, g_ref[...]).astype(BF16), wq_ref[...].astype(BF16)).astype(BF16)
    heads = []
    for h in range(n_heads):
        sl = slice(h * dh, (h + 1) * dh)
        s = _dot_nt(q[:, sl], k_ref[:, sl]) * (dh ** -0.5)
        m = jnp.max(s, axis=-1, keepdims=True)
        p = jnp.exp(s - m)
        p = p / jnp.sum(p, axis=-1, keepdims=True)
        heads.append(_dot(p.astype(BF16), v_ref[:, sl]).astype(BF16))
    o_ref[...] = x + _dot(jnp.concatenate(heads, axis=1), wo_ref[...].astype(BF16))


def cross_block_prompt(x, g, wq, mkv, wo, layer, *, batch, seq, n_mem, n_heads, dh):
    d = x.shape[1]
    w = n_heads * dh
    tq = _pick(seq, (1024, 512, 256, 128))
    nq = seq // tq
    row_blk = pl.BlockSpec((tq, d), lambda b, i: (b * nq + i, 0))
    return pl.pallas_call(
        functools.partial(_cross_block_kernel, n_heads=n_heads, dh=dh),
        grid=(batch, nq),
        in_specs=[row_blk,
                  pl.BlockSpec((1, d), lambda b, i: (0, 0)),
                  pl.BlockSpec((None, d, w), lambda b, i: (layer, 0, 0)),
                  pl.BlockSpec((n_mem, w), lambda b, i: (b, 0)),
                  pl.BlockSpec((n_mem, w), lambda b, i: (b, 1)),
                  pl.BlockSpec((None, w, d), lambda b, i: (layer, 0, 0))],
        out_specs=row_blk,
        out_shape=jax.ShapeDtypeStruct(x.shape, F32),
        compiler_params=_params("parallel", "parallel"),
        name="cross_block_prompt",
    )(x, g.reshape(1, d), wq, mkv, mkv, wo)


def _diff_decode_kernel(*refs, n_pages, page, n_heads, dh, lam_init):
    pos = 1
    qt_ref, knt_ref, vn_ref = refs[pos:pos + 3]; pos += 3
    kt_refs = refs[pos:pos + n_pages]; pos += n_pages
    v_refs = refs[pos:pos + n_pages]; pos += n_pages
    slopes_ref = refs[pos]; pos += 1
    lparams = refs[pos:pos + 4]; pos += 4
    subln_ref = refs[pos]; pos += 1
    o_ref = refs[pos]

    b = pl.program_id(0)
    nb = qt_ref.shape[1]
    onehot = (lax.broadcasted_iota(jnp.int32, (nb, V7X_LANES), 0) == b).astype(BF16)
    q_rep = _dot(qt_ref[...], onehot) * (dh ** -0.5)
    kn_rep = _dot(knt_ref[...], onehot)

    def map_sums(x):
        x3 = x.reshape(n_heads, 2 * dh, x.shape[-1])
        return x3[:, :dh, :].sum(axis=1), x3[:, dh:, :].sum(axis=1)

    past = n_pages * page
    tpos = lax.broadcasted_iota(jnp.int32, (n_heads, page), 1).astype(F32)
    nslope = -slopes_ref[...]
    s1, s2 = [], []
    for p in range(n_pages):
        a, c = map_sums(kt_refs[p][...] * q_rep)
        bias = nslope * ((past - p * page) - tpos)
        s1.append(a + bias)
        s2.append(c + bias)
    n1, n2 = map_sums(kn_rep * q_rep)

    def softmax_parts(s_list, s_new):
        m = s_list[0]
        for s in s_list[1:]:
            m = jnp.maximum(m, s)
        m = jnp.maximum(jnp.max(m, axis=-1, keepdims=True), s_new)
        e = [jnp.exp(s - m) for s in s_list]
        e_new = jnp.exp(s_new - m)
        l = e[0]
        for x in e[1:]:
            l = l + x
        l = jnp.sum(l, axis=-1, keepdims=True) + e_new
        return e, e_new, 1.0 / l

    e1, en1, a1 = softmax_parts(s1, n1)
    e2, en2, a2 = softmax_parts(s2, n2)
    a2 = a2 * _diff_lambda(*lparams, lam_init)

    tok = lax.broadcasted_iota(jnp.int32, (page, n_heads, V7X_LANES), 0)
    lane = lax.broadcasted_iota(jnp.int32, (page, n_heads, V7X_LANES), 2)
    diag = tok == lane
    ones = jnp.ones((V7X_LANES, V7X_LANES), BF16)
    o = (en1 * a1 - en2 * a2) * vn_ref[...]
    for p in range(n_pages):
        w = e1[p] * a1 - e2[p] * a2
        wd = jnp.where(diag, w[None], 0.0).reshape(page * n_heads, V7X_LANES).astype(BF16)
        w3 = _dot(wd, ones).reshape(page, n_heads, V7X_LANES)
        o = o + jnp.sum(w3 * v_refs[p][...], axis=0)
    o_ref[...] = (_rms(o, subln_ref[...]) * (1.0 - lam_init)).astype(o_ref.dtype)


def diff_attn_sample(q, k_new, v_new, cache_kt, cache_v, layer, page_table, slopes, lparams, subln, *,
                     n_heads, dh, dv, lam_init):
    nb, n_pages = page_table.shape
    page = cache_v.shape[2]
    wk = n_heads * 2 * dh
    assert page == V7X_LANES and dv == V7X_LANES and n_heads == V7X_SUBLANES
    const = lambda shape: pl.BlockSpec(shape, lambda b, pt: (0,) * len(shape))

    def kt_spec(p):
        return pl.BlockSpec((None, None, wk, page), lambda b, pt: (layer, pt[b, p], 0, 0))

    def v_spec(p):
        return pl.BlockSpec((None, None, page, n_heads, dv), lambda b, pt: (layer, pt[b, p], 0, 0, 0))

    in_specs = ([const((wk, nb)), const((wk, nb)),
                 pl.BlockSpec((None, n_heads, dv), lambda b, pt: (b, 0, 0))]
                + [kt_spec(p) for p in range(n_pages)] + [v_spec(p) for p in range(n_pages)]
                + [const((n_heads, V7X_LANES))] + [const((1, dh))] * 4 + [const((1, dv))])
    out = pl.pallas_call(
        functools.partial(_diff_decode_kernel, n_pages=n_pages, page=page, n_heads=n_heads, dh=dh,
                          lam_init=lam_init),
        grid_spec=pltpu.PrefetchScalarGridSpec(
            num_scalar_prefetch=1, grid=(nb,), in_specs=in_specs,
            out_specs=pl.BlockSpec((None, n_heads, dv), lambda b, pt: (b, 0, 0))),
        out_shape=jax.ShapeDtypeStruct((nb, n_heads, dv), BF16),
        compiler_params=_params("arbitrary"),
        name="diff_attn_sample",
    )(page_table, q.T.astype(BF16), k_new.T.astype(BF16), v_new.reshape(nb, n_heads, dv),
      *([cache_kt] * n_pages), *([cache_v] * n_pages),
      jnp.broadcast_to(slopes[:, None], (n_heads, V7X_LANES)), *lparams, subln)
    return out.reshape(nb, n_heads * dv)


def _cross_decode_kernel(q_ref, k_ref, v_ref, o_ref, *, n_heads, dh):
    n_req, n_tiles = k_ref.shape[:2]
    ones = jnp.ones((dh, V7X_LANES), BF16)

    def over_tokens(x, op):
        out = x
        for r in range(n_heads, V7X_SUBLANES, n_heads):
            out = op(out, pltpu.roll(x, r, axis=0))
        return out

    for r in range(n_req):
        q = q_ref[r] * (dh ** -0.5)
        prod = (k_ref[r] * q[None]).reshape(n_tiles * V7X_SUBLANES, dh).astype(BF16)
        s = _dot(prod, ones).reshape(n_tiles, V7X_SUBLANES, V7X_LANES)
        m = over_tokens(jnp.max(s, axis=0), jnp.maximum)
        e = jnp.exp(s - m[None])
        l = over_tokens(jnp.sum(e, axis=0), jnp.add)
        o = over_tokens(jnp.sum(e * v_ref[r], axis=0), jnp.add)
        o_ref[r] = o / l


def cross_attn_sample(q, cache_k, cache_v, layer, *, n_heads, dh):
    nb = q.shape[0]
    assert dh == V7X_LANES and V7X_SUBLANES % n_heads == 0
    n_tiles = cache_k.shape[2]
    q8 = jnp.tile(q.reshape(nb, n_heads, dh), (1, V7X_SUBLANES // n_heads, 1))
    rb = _pick(nb, (8, 4, 2, 1))
    kv_spec = pl.BlockSpec((None, rb, n_tiles, V7X_SUBLANES, dh), lambda b: (layer, b, 0, 0, 0))
    row = pl.BlockSpec((rb, V7X_SUBLANES, dh), lambda b: (b, 0, 0))
    out = pl.pallas_call(
        functools.partial(_cross_decode_kernel, n_heads=n_heads, dh=dh),
        grid=(nb // rb,),
        in_specs=[row, kv_spec, kv_spec],
        out_specs=row,
        out_shape=jax.ShapeDtypeStruct((nb, V7X_SUBLANES, dh), F32),
        compiler_params=_params("parallel"),
        name="cross_attn_sample",
    )(q8, cache_k, cache_v)
    return out[:, :n_heads].reshape(nb, n_heads * dh)


def _retention_step_kernel(q_ref, k_ref, v_ref, s_ref, *rest, n_heads, dk, dv, log_gammas):
    o_ref, sn_ref = rest[-2:]
    scale = dk ** -0.5
    eye = (lax.broadcasted_iota(jnp.int32, (dk, dk), 0)
           == lax.broadcasted_iota(jnp.int32, (dk, dk), 1))
    for r in range(q_ref.shape[0]):
        for h in range(n_heads):
            gamma = math.exp(log_gammas[h])
            q = q_ref[r, :, h * dk:(h + 1) * dk]
            k = k_ref[r, :, h * dk:(h + 1) * dk]
            v = v_ref[r, :, h * dv:(h + 1) * dv]
            state = s_ref[r, h]
            inner = jnp.sum(q * k, axis=-1, keepdims=True) * scale
            q8 = jnp.broadcast_to(q, (V7X_SUBLANES, dk)).astype(BF16)
            o_cross = _dot(q8, state.astype(BF16))[0:1] * gamma
            o_ref[r, :, h * dv:(h + 1) * dv] = inner * v + o_cross
            k_diag = jnp.where(eye, jnp.broadcast_to(k * scale, (dk, dk)), 0.0).astype(BF16)
            v_rows = jnp.broadcast_to(v, (dk, dv)).astype(BF16)
            sn_ref[r, h] = gamma * state + _dot(k_diag, v_rows)


def retention_sample(q, k, v, state, layer, stacked, *, n_heads, dk, dv, log_gammas):
    nb = q.shape[0]
    wq, wv = n_heads * dk, n_heads * dv
    rb = _pick(nb, (8, 4, 2, 1))
    row = lambda w: pl.BlockSpec((rb, 1, w), lambda b, *_: (b, 0, 0))
    state_blk = pl.BlockSpec((None, rb, n_heads, dk, dv), lambda b, *_: (layer, b, 0, 0, 0))
    ins = [q.reshape(nb, 1, wq), k.reshape(nb, 1, wq), v.reshape(nb, 1, wv), state]
    in_specs = [row(wq), row(wq), row(wv), state_blk]
    aliases = _stacked(in_specs, ins, stacked, n_out_before=1)
    o, s_new = pl.pallas_call(
        functools.partial(_retention_step_kernel, n_heads=n_heads, dk=dk, dv=dv,
                          log_gammas=log_gammas),
        grid=(nb // rb,),
        in_specs=in_specs,
        out_specs=(row(wv), state_blk),
        out_shape=(jax.ShapeDtypeStruct((nb, 1, wv), F32),
                   jax.ShapeDtypeStruct(state.shape, F32)),
        input_output_aliases=aliases,
        compiler_params=_params("parallel"),
        name="retention_sample",
    )(*ins)
    return o.reshape(nb, wv), s_new


def kernel(x_prompt, x_sample, mem_prompt, cache_k_diff, cache_v_diff, cache_mem_k, cache_mem_v,
           state_ret, page_table, norm_mix, w_in, w_out, da_lq1, da_lk1, da_lq2, da_lk2, da_subln,
           ret_norm, norm_x, norm_mem, wx_q, wx_kv, wx_o, norm_ff, w_up, w_down, norm_final):
    batch, seq, d_model = x_prompt.shape
    dec_batch, dec_seq, _ = x_sample.shape
    assert dec_seq == 1
    depth, n_pool, page, h_da, _, dh_da = cache_k_diff.shape
    dv_da = cache_v_diff.shape[-1]
    _, _, h_ret, dk_ret, dv_ret = state_ret.shape
    _, _, n_mem, h_x, dh_x = cache_mem_k.shape
    d_qk = h_da * 2 * dh_da
    d_da = h_da * dv_da
    d_rqk = h_ret * dk_ret
    d_ret = h_ret * dv_ret
    d_x = h_x * dh_x
    chunk = _pick(seq, (256, 128))
    mp = batch * seq

    slopes = 2.0 ** (-8.0 * (jnp.arange(h_da, dtype=F32) + 1.0) / h_da)
    log_gammas = tuple(math.log(1.0 - 2.0 ** (-5.0 - h)) for h in range(h_ret))

    o_q, o_k, o_v = 0, d_qk, 2 * d_qk
    o_qr = o_v + d_da
    o_kr, o_vr = o_qr + d_rqk, o_qr + 2 * d_rqk
    o_g = o_vr + d_ret

    cache_kt = jnp.transpose(cache_k_diff, (0, 1, 3, 4, 5, 2)).reshape(depth, n_pool, d_qk, page)
    mem_tiles = n_mem * h_x // V7X_SUBLANES
    mem_k = cache_mem_k.reshape(depth, dec_batch, mem_tiles, V7X_SUBLANES, dh_x)
    mem_v = cache_mem_v.reshape(depth, dec_batch, mem_tiles, V7X_SUBLANES, dh_x)

    xp = x_prompt.reshape(mp, d_model)
    xs = x_sample.reshape(dec_batch, d_model)
    mem = mem_prompt.reshape(batch * n_mem, d_model)

    wu_all, wd_all = w_up.astype(BF16), w_down.astype(BF16)
    outs = {name: [] for name in ("ps", "pmk", "pmv", "sk", "sv")}
    pk_all = pv_all = ss_all = None
    for l in range(depth):
        lam_init = 0.8 - 0.6 * math.exp(-0.3 * l)
        lparams = tuple(a[l].reshape(1, dh_da) for a in (da_lq1, da_lk1, da_lq2, da_lk2))
        subln = da_subln[l].reshape(1, dv_da)
        seg = lambda c0, c1: ((c0, c1 - c0),)
        w_a = WView(w_in, l, 0, d_model, seg(o_q, o_k) + seg(o_qr, o_g))
        w_k = WView(w_in, l, 0, d_model, seg(o_k, o_v))
        w_v = WView(w_in, l, 0, d_model, seg(o_v, o_qr))
        w_g = WView(w_in, l, 0, d_model, seg(o_g, o_g + d_ret))
        w_out_da = WView(w_out, l, 0, d_da, seg(0, d_model))
        w_out_r = WView(w_out, l, d_da, d_ret, seg(0, d_model))
        wq_b = WView(wx_q, l, 0, d_model, seg(0, d_x))
        wkv_b = WView(wx_kv, l, 0, d_model, seg(0, 2 * d_x))
        wo_b = WView(wx_o, l, 0, d_x, seg(0, d_model))
        a_q, a_qr, a_kr, a_vr = 0, d_qk, d_qk + d_rqk, d_qk + 2 * d_rqk

        last = l == depth - 1
        g_next = norm_final if last else norm_mix[l + 1]

        if l == 0:
            pk_all, k_b, h = kproj(xp, w_k, l, depth, None, batch=batch, seq=seq, norm_gain=norm_mix[l])
        else:
            h = hp_next
            pk_all, k_b = kproj(h, w_k, l, depth, pk_all, batch=batch, seq=seq)
        (pa,) = matmul([(h, w_a)], (BF16,))
        pv_all, v_b = vproj(h, w_v, l, depth, pv_all, n_heads=h_da)
        (g_r,) = matmul([(h, w_g)], (F32,))
        y_da = diff_attn_prompt(pa, k_b, v_b, slopes, lparams, subln,
                                batch=batch, seq=seq, n_heads=h_da, dh=dh_da, dv=dv_da, lam_init=lam_init)
        y_r, s_p = retention_prompt(pa, g_r, ret_norm[l], batch=batch, seq=seq, n_heads=h_ret,
                                    dk=dk_ret, dv=dv_ret, chunk=chunk, log_gammas=log_gammas,
                                    q_col=a_qr, k_col=a_kr, v_col=a_vr)
        (xp,) = matmul([(y_da, w_out_da), (y_r, w_out_r)], (F32,), residual=xp)
        hm = rmsnorm(mem, norm_mem[l], BF16)
        mkv_f32, mkv_b = matmul([(hm, wkv_b)], (F32, BF16))
        xp = cross_block_prompt(xp, norm_x[l], wx_q, mkv_b, wx_o, l, batch=batch, seq=seq, n_mem=n_mem,
                                n_heads=h_x, dh=dh_x)
        if last:
            (y_prompt,) = mlp(xp, norm_ff[l], wu_all, wd_all, l, g_next, keep_x=False)
        else:
            xp, hp_next = mlp(xp, norm_ff[l], wu_all, wd_all, l, g_next, keep_x=True)
        outs["ps"].append(s_p)
        outs["pmk"].append(mkv_f32[:, :d_x].reshape(batch, n_mem, h_x, dh_x))
        outs["pmv"].append(mkv_f32[:, d_x:].reshape(batch, n_mem, h_x, dh_x))

        h = rmsnorm(xs, norm_mix[l], BF16) if l == 0 else hs_next
        (ps_a,) = matmul([(h, w_a)], (F32,))
        (k_new,) = matmul([(h, w_k)], (F32,))
        (v_new,) = matmul([(h, w_v)], (F32,))
        (g_s,) = matmul([(h, w_g)], (F32,))
        y_da = diff_attn_sample(ps_a[:, a_q:a_qr], k_new, v_new, cache_kt, cache_v_diff, l, page_table,
                                slopes, lparams, subln, n_heads=h_da, dh=dh_da, dv=dv_da,
                                lam_init=lam_init)
        o_r, ss_all = retention_sample(ps_a[:, a_qr:a_kr], ps_a[:, a_kr:a_vr], ps_a[:, a_vr:], state_ret, l, ss_all,
                                    n_heads=h_ret, dk=dk_ret, dv=dv_ret, log_gammas=log_gammas)
        y_r = gated_headnorm(o_r.reshape(dec_batch * h_ret, dv_ret), ret_norm[l],
                             g_s.reshape(dec_batch * h_ret, dv_ret))
        (xs,) = matmul([(y_da, w_out_da), (y_r.reshape(dec_batch, d_ret), w_out_r)],
                       (F32,), residual=xs)
        hx = rmsnorm(xs, norm_x[l], BF16)
        (qx,) = matmul([(hx, wq_b)], (F32,))
        ox = cross_attn_sample(qx, mem_k, mem_v, l, n_heads=h_x, dh=dh_x)
        (xs,) = matmul([(ox.astype(BF16), wo_b)], (F32,), residual=xs)
        if last:
            (y_sample,) = mlp(xs, norm_ff[l], wu_all, wd_all, l, g_next, keep_x=False)
        else:
            xs, hs_next = mlp(xs, norm_ff[l], wu_all, wd_all, l, g_next, keep_x=True)
        outs["sk"].append(k_new.reshape(dec_batch, 1, h_da, 2, dh_da))
        outs["sv"].append(v_new.reshape(dec_batch, 1, h_da, dv_da))

    y_prompt = y_prompt.reshape(batch, seq, d_model)
    y_sample = y_sample.reshape(dec_batch, 1, d_model)
    st = lambda name: jnp.stack(outs[name])
    new_k_prompt = jnp.transpose(pk_all.reshape(depth, batch, h_da, 2, dh_da, seq), (0, 1, 5, 2, 3, 4))
    return (y_prompt, y_sample, new_k_prompt, pv_all.reshape(depth, batch, seq, h_da, dv_da), st("ps"),
            st("pmk"), st("pmv"), st("sk"), st("sv"), ss_all)
```
